```python
import math
import jax, jax.numpy as jnp
from jax import lax
import numpy as np

D_MODEL = 1024
BATCH = 16
SEQ = 2048
DEPTH = 4
DEC_BATCH = 128
DEC_SEQ = 4
PAST_LEN = 8192
PAGE_SIZE = 128

N_EVEN = (DEPTH + 1) // 2
N_ODD = DEPTH // 2

POOL_DIM = D_MODEL // 2
POOL_WINDOWS = (2, 4, 8, 16)
POOL_GROUPS = len(POOL_WINDOWS)
POOL_GROUP_DIM = POOL_DIM // POOL_GROUPS
POOL_PAD = max(POOL_WINDOWS)
POOL_STATE = max(POOL_WINDOWS) - 1

MLA_HEADS = 8
MLA_NOPE = 64
MLA_ROPE = 32
MLA_V = (D_MODEL - POOL_DIM) // MLA_HEADS
MLA_Q_RANK = D_MODEL // 4
MLA_KV_RANK = D_MODEL // 8
MLA_SCALE = (MLA_NOPE + MLA_ROPE) ** -0.5
ROPE_THETA = 10000.0
EVEN_IN = POOL_DIM + MLA_Q_RANK + MLA_KV_RANK + MLA_ROPE
EVEN_OUT = POOL_DIM + MLA_HEADS * MLA_V

DIFF_HEADS = 8
DIFF_HEAD_DIM = D_MODEL // (2 * DIFF_HEADS)
DIFF_KV_HEADS = 2
DIFF_GROUP = DIFF_HEADS // DIFF_KV_HEADS
DIFF_SCALE = DIFF_HEAD_DIM ** -0.5
DIFF_Q = DIFF_HEADS * 2 * DIFF_HEAD_DIM
DIFF_KV = DIFF_KV_HEADS * 2 * DIFF_HEAD_DIM
ODD_IN = DIFF_Q + 2 * DIFF_KV
ODD_OUT = DIFF_HEADS * 2 * DIFF_HEAD_DIM

D_FF = 4 * D_MODEL
ALPHA = (2 * DEPTH) ** 0.25
BETA = (8 * DEPTH) ** -0.25
EPS = 1e-5
Q_BLOCK = 128

kernel_name = 'hybrid_pool_mla_diffattn_step'


def layer_norm(x, g, b):
    xf = x.astype(jnp.float32)
    mu = jnp.mean(xf, -1, keepdims=True)
    var = jnp.mean(jnp.square(xf - mu), -1, keepdims=True)
    return ((xf - mu) * lax.rsqrt(var + EPS) * g + b).astype(x.dtype)


def rms_norm(x, g):
    xf = x.astype(jnp.float32)
    return (xf * lax.rsqrt(jnp.mean(xf * xf, -1, keepdims=True) + EPS) * g).astype(x.dtype)


def rope(x, pos):
    half = x.shape[-1] // 2
    freq = ROPE_THETA ** (-jnp.arange(half, dtype=jnp.float32) / half)
    ang = pos[:, None] * freq[None, :]
    shape = (pos.shape[0],) + (1,) * (x.ndim - 3) + (half,)
    cos = jnp.cos(ang).reshape(shape)
    sin = jnp.sin(ang).reshape(shape)
    xf = x.astype(jnp.float32)
    x1, x2 = xf[..., :half], xf[..., half:]
    return jnp.concatenate([x1 * cos - x2 * sin, x1 * sin + x2 * cos], -1).astype(x.dtype)


def causal_block_sweep(block_fn, queries):
    b, s = queries[0].shape[:2]
    nb = s // Q_BLOCK
    qb = tuple(jnp.moveaxis(q.reshape((b, nb, Q_BLOCK) + q.shape[2:]), 1, 0) for q in queries)
    kpos = jnp.arange(s)

    def body(args):
        qs, i = args[:-1], args[-1]
        qpos = i * Q_BLOCK + jnp.arange(Q_BLOCK)
        return block_fn(qs, kpos[None, :] <= qpos[:, None])

    out = lax.map(body, qb + (jnp.arange(nb),))
    return jnp.moveaxis(out, 0, 1).reshape((b, s) + out.shape[3:])


def pool_mix(u_ext, start_pos, n_out, w_grp, scale):
    b, L = u_ext.shape[:2]
    uf = u_ext.astype(jnp.float32)
    cs = jnp.pad(jnp.cumsum(uf, axis=1), ((0, 0), (POOL_PAD, 0), (0, 0)))
    pos = start_pos + jnp.arange(L - n_out, L, dtype=jnp.float32)
    diffs = []
    for g, w in enumerate(POOL_WINDOWS):
        c = cs[..., g * POOL_GROUP_DIM:(g + 1) * POOL_GROUP_DIM]
        hi = c[:, POOL_PAD + L - n_out:POOL_PAD + L]
        lo = c[:, POOL_PAD + L - n_out - w:POOL_PAD + L - w]
        cnt = jnp.minimum(pos + 1.0, float(w))
        mean = (hi - lo) / cnt[None, :, None]
        diffs.append(mean - uf[:, L - n_out:, g * POOL_GROUP_DIM:(g + 1) * POOL_GROUP_DIM])
    d = jnp.stack(diffs, axis=2).astype(u_ext.dtype)
    y = jnp.einsum('bngc,gcd->bngd', d, w_grp)
    return y.reshape(b, n_out, POOL_DIM) * scale


def even_project(x, pos, w_in, q_norm, w_uq, kv_norm, w_uk):
    b, t = x.shape[:2]
    h = x @ w_in
    u, cq, ckv, kr = jnp.split(h, [POOL_DIM, POOL_DIM + MLA_Q_RANK, POOL_DIM + MLA_Q_RANK + MLA_KV_RANK], axis=-1)
    q = (rms_norm(cq, q_norm) @ w_uq).reshape(b, t, MLA_HEADS, MLA_NOPE + MLA_ROPE)
    q_rope = rope(q[..., MLA_NOPE:], pos)
    q_lat = jnp.einsum('bthn,rhn->bthr', q[..., :MLA_NOPE], w_uk)
    return u, q_lat, q_rope, rms_norm(ckv, kv_norm), rope(kr, pos)


def mla_scores(q_lat, q_rope, ckv, kr):
    s = jnp.einsum('bthr,bsr->bhts', q_lat, ckv) + jnp.einsum('bthe,bse->bhts', q_rope, kr)
    return s.astype(jnp.float32) * MLA_SCALE


def mla_prompt(q_lat, q_rope, ckv, kr):
    def blk(qs, mask):
        ql, qr = qs
        s = jnp.where(mask, mla_scores(ql, qr, ckv, kr), -jnp.inf)
        p = jax.nn.softmax(s, axis=-1).astype(ckv.dtype)
        return jnp.einsum('bhts,bsr->bthr', p, ckv)
    return causal_block_sweep(blk, (q_lat, q_rope))


def mla_sample(q_lat, q_rope, ckv_past, kr_past, ckv_new, kr_new):
    t = q_lat.shape[1]
    causal = jnp.tril(jnp.ones((t, t), dtype=bool))
    s_new = jnp.where(causal, mla_scores(q_lat, q_rope, ckv_new, kr_new), -jnp.inf)
    s = jnp.concatenate([mla_scores(q_lat, q_rope, ckv_past, kr_past), s_new], -1)
    p = jax.nn.softmax(s, axis=-1).astype(ckv_new.dtype)
    n_past = ckv_past.shape[1]
    return (jnp.einsum('bhts,bsr->bthr', p[..., :n_past], ckv_past)
            + jnp.einsum('bhts,bsr->bthr', p[..., n_past:], ckv_new))


def even_output(pool_out, o_lat, w_uv, w_out):
    b, t = pool_out.shape[:2]
    o = jnp.einsum('bthr,rhv->bthv', o_lat, w_uv).reshape(b, t, MLA_HEADS * MLA_V)
    return jnp.concatenate([pool_out.astype(o.dtype), o], -1) @ w_out


def diff_project(x, w_in):
    b, t = x.shape[:2]
    h = x @ w_in
    q = h[..., :DIFF_Q].reshape(b, t, DIFF_KV_HEADS, DIFF_GROUP, 2, DIFF_HEAD_DIM)
    k = h[..., DIFF_Q:DIFF_Q + DIFF_KV].reshape(b, t, DIFF_KV_HEADS, 2 * DIFF_HEAD_DIM)
    v = h[..., DIFF_Q + DIFF_KV:].reshape(b, t, DIFF_KV_HEADS, 2 * DIFF_HEAD_DIM)
    return q, k, v


def diff_scores(q, k):
    kk = k.reshape(k.shape[:3] + (2, DIFF_HEAD_DIM))
    return jnp.einsum('btkgmd,bskmd->bkgmts', q, kk).astype(jnp.float32) * DIFF_SCALE


def diff_weights(p, lam):
    return p[:, :, :, 0] - lam * p[:, :, :, 1]


def diff_prompt(q, k, v, lam):
    def blk(qs, mask):
        s = jnp.where(mask, diff_scores(qs[0], k), -jnp.inf)
        a = diff_weights(jax.nn.softmax(s, axis=-1), lam).astype(v.dtype)
        return jnp.einsum('bkgts,bskv->btkgv', a, v)
    return causal_block_sweep(blk, (q,))


def diff_sample(q, k_past, v_past, k_new, v_new, lam):
    t = q.shape[1]
    causal = jnp.tril(jnp.ones((t, t), dtype=bool))
    s = jnp.concatenate([diff_scores(q, k_past), jnp.where(causal, diff_scores(q, k_new), -jnp.inf)], -1)
    a = diff_weights(jax.nn.softmax(s, axis=-1), lam).astype(v_new.dtype)
    n_past = k_past.shape[1]
    return (jnp.einsum('bkgts,bskv->btkgv', a[..., :n_past], v_past)
            + jnp.einsum('bkgts,bskv->btkgv', a[..., n_past:], v_new))


def diff_lambda(lq1, lk1, lq2, lk2, lam_init):
    return (jnp.exp(jnp.sum(lq1.astype(jnp.float32) * lk1.astype(jnp.float32)))
            - jnp.exp(jnp.sum(lq2.astype(jnp.float32) * lk2.astype(jnp.float32))) + lam_init)


def diff_output(o, norm_g, lam_init, w_out):
    b, t = o.shape[:2]
    o = rms_norm(o, norm_g) * (1.0 - lam_init)
    return o.reshape(b, t, ODD_OUT) @ w_out


def post_block(x, mix, g1, bn1, g2, bn2, w1, b1, w2, b2):
    x = layer_norm(ALPHA * x + mix, g1, bn1)
    ffn = jnp.square(jax.nn.relu(x @ w1 + b1)) @ w2 + b2
    return layer_norm(ALPHA * x + ffn, g2, bn2)


def setup_inputs(seed: int = 0) -> dict:
    key = jax.random.key(seed)
    ks = iter(jax.random.split(key, 64))

    def nrm(shape, scale):
        return jax.random.normal(next(ks), shape, jnp.float32) * scale

    def gain(shape):
        return 1.0 + nrm(shape, 0.1)

    n_pages = PAST_LEN // PAGE_SIZE
    n_used = DEC_BATCH * n_pages
    n_pool = n_used + n_used // 4
    d = D_MODEL
    inputs = {
        'x_prompt': nrm((BATCH, SEQ, d), 1.0),
        'x_sample': nrm((DEC_BATCH, DEC_SEQ, d), 1.0),
        'cache_mla_ckv': nrm((N_EVEN, n_pool, PAGE_SIZE, MLA_KV_RANK), 1.0),
        'cache_mla_krope': nrm((N_EVEN, n_pool, PAGE_SIZE, MLA_ROPE), 1.0),
        'cache_diff_k': nrm((N_ODD, n_pool, PAGE_SIZE, DIFF_KV_HEADS, 2 * DIFF_HEAD_DIM), 1.0),
        'cache_diff_v': nrm((N_ODD, n_pool, PAGE_SIZE, DIFF_KV_HEADS, 2 * DIFF_HEAD_DIM), 1.0),
        'state_pool': nrm((N_EVEN, DEC_BATCH, POOL_STATE, POOL_DIM), 1.0),
        'page_table': jax.random.permutation(next(ks), n_pool)[:n_used].reshape(DEC_BATCH, n_pages).astype(jnp.int32),
        'w_in_even': nrm((N_EVEN, d, EVEN_IN), d ** -0.5),
        'pool_w': nrm((N_EVEN, POOL_GROUPS, POOL_GROUP_DIM, POOL_GROUP_DIM), POOL_GROUP_DIM ** -0.5),
        'pool_scale': gain((N_EVEN, POOL_DIM)),
        'mla_q_norm': gain((N_EVEN, MLA_Q_RANK)),
        'mla_w_uq': nrm((N_EVEN, MLA_Q_RANK, MLA_HEADS * (MLA_NOPE + MLA_ROPE)), MLA_Q_RANK ** -0.5),
        'mla_kv_norm': gain((N_EVEN, MLA_KV_RANK)),
        'mla_w_uk': nrm((N_EVEN, MLA_KV_RANK, MLA_HEADS, MLA_NOPE), MLA_KV_RANK ** -0.5),
        'mla_w_uv': nrm((N_EVEN, MLA_KV_RANK, MLA_HEADS, MLA_V), MLA_KV_RANK ** -0.5),
        'w_out_even': nrm((N_EVEN, EVEN_OUT, d), BETA * EVEN_OUT ** -0.5),
        'w_in_odd': nrm((N_ODD, d, ODD_IN), d ** -0.5),
        'diff_lq1': nrm((N_ODD, DIFF_HEAD_DIM), 0.1),
        'diff_lk1': nrm((N_ODD, DIFF_HEAD_DIM), 0.1),
        'diff_lq2': nrm((N_ODD, DIFF_HEAD_DIM), 0.1),
        'diff_lk2': nrm((N_ODD, DIFF_HEAD_DIM), 0.1),
        'diff_norm': gain((N_ODD, 2 * DIFF_HEAD_DIM)),
        'w_out_odd': nrm((N_ODD, ODD_OUT, d), BETA * ODD_OUT ** -0.5),
        'ln1_g': gain((DEPTH, d)),
        'ln1_b': nrm((DEPTH, d), 0.02),
        'ln2_g': gain((DEPTH, d)),
        'ln2_b': nrm((DEPTH, d), 0.02),
        'mlp_w1': nrm((DEPTH, d, D_FF), d ** -0.5),
        'mlp_b1': nrm((DEPTH, D_FF), 0.02),
        'mlp_w2': nrm((DEPTH, D_FF, d), BETA * D_FF ** -0.5),
        'mlp_b2': nrm((DEPTH, d), 0.02),
    }
    return inputs


def reference(x_prompt, x_sample, cache_mla_ckv, cache_mla_krope, cache_diff_k, cache_diff_v, state_pool,
              page_table, w_in_even, pool_w, pool_scale, mla_q_norm, mla_w_uq, mla_kv_norm, mla_w_uk,
              mla_w_uv, w_out_even, w_in_odd, diff_lq1, diff_lk1, diff_lq2, diff_lk2, diff_norm, w_out_odd,
              ln1_g, ln1_b, ln2_g, ln2_b, mlp_w1, mlp_b1, mlp_w2, mlp_b2):
    s_len = x_prompt.shape[1]
    db, t_len = x_sample.shape[:2]
    past = page_table.shape[1] * PAGE_SIZE
    pos_p = jnp.arange(s_len, dtype=jnp.float32)
    pos_s = past + jnp.arange(t_len, dtype=jnp.float32)

    xp, xs = x_prompt, x_sample
    p_ckv, p_kr, p_dk, p_dv, p_pool = [], [], [], [], []
    s_ckv, s_kr, s_dk, s_dv, s_pool = [], [], [], [], []
    for i in range(DEPTH):
        if i % 2 == 0:
            e = i // 2
            proj = (w_in_even[e], mla_q_norm[e], mla_w_uq[e], mla_kv_norm[e], mla_w_uk[e])
            u, ql, qr, ckv, kr = even_project(xp, pos_p, *proj)
            pool_o = pool_mix(u, 0, s_len, pool_w[e], pool_scale[e])
            mix_p = even_output(pool_o, mla_prompt(ql, qr, ckv, kr), mla_w_uv[e], w_out_even[e])
            p_ckv.append(ckv)
            p_kr.append(kr)
            p_pool.append(u[:, s_len - POOL_STATE:])
            u, ql, qr, ckv, kr = even_project(xs, pos_s, *proj)
            u_ext = jnp.concatenate([state_pool[e].astype(u.dtype), u], axis=1)
            pool_o = pool_mix(u_ext, past - POOL_STATE, t_len, pool_w[e], pool_scale[e])
            ckv_past = cache_mla_ckv[e, page_table].reshape(db, past, MLA_KV_RANK)
            kr_past = cache_mla_krope[e, page_table].reshape(db, past, MLA_ROPE)
            o_lat = mla_sample(ql, qr, ckv_past.astype(ckv.dtype), kr_past.astype(kr.dtype), ckv, kr)
            mix_s = even_output(pool_o, o_lat, mla_w_uv[e], w_out_even[e])
            s_ckv.append(ckv)
            s_kr.append(kr)
            s_pool.append(u_ext[:, u_ext.shape[1] - POOL_STATE:])
        else:
            o = i // 2
            lam_init = 0.8 - 0.6 * math.exp(-0.3 * i)
            lam = diff_lambda(diff_lq1[o], diff_lk1[o], diff_lq2[o], diff_lk2[o], lam_init)
            q, k, v = diff_project(xp, w_in_odd[o])
            mix_p = diff_output(diff_prompt(q, k, v, lam), diff_norm[o], lam_init, w_out_odd[o])
            p_dk.append(k)
            p_dv.append(v)
            q, k, v = diff_project(xs, w_in_odd[o])
            k_past = cache_diff_k[o, page_table].reshape(db, past, DIFF_KV_HEADS, 2 * DIFF_HEAD_DIM)
            v_past = cache_diff_v[o, page_table].reshape(db, past, DIFF_KV_HEADS, 2 * DIFF_HEAD_DIM)
            att = diff_sample(q, k_past.astype(k.dtype), v_past.astype(v.dtype), k, v, lam)
            mix_s = diff_output(att, diff_norm[o], lam_init, w_out_odd[o])
            s_dk.append(k)
            s_dv.append(v)
        ffn = (ln1_g[i], ln1_b[i], ln2_g[i], ln2_b[i], mlp_w1[i], mlp_b1[i], mlp_w2[i], mlp_b2[i])
        xp = post_block(xp, mix_p, *ffn)
        xs = post_block(xs, mix_s, *ffn)

    new_p_ckv = jnp.stack(p_ckv)
    new_p_kr = jnp.stack(p_kr)
    new_p_dk = jnp.stack(p_dk)
    new_p_dv = jnp.stack(p_dv)
    new_p_pool = jnp.stack(p_pool)
    new_s_ckv = jnp.stack(s_ckv)
    new_s_kr = jnp.stack(s_kr)
    new_s_dk = jnp.stack(s_dk)
    new_s_dv = jnp.stack(s_dv)
    new_s_pool = jnp.stack(s_pool)
    return (xp, xs, new_p_ckv, new_p_kr, new_p_dk, new_p_dv, new_p_pool,
            new_s_ckv, new_s_kr, new_s_dk, new_s_dv, new_s_pool)
```

```python
import functools
import math

import jax
import jax.numpy as jnp
from jax import lax
from jax.experimental import pallas as pl
from jax.experimental.pallas import tpu as pltpu

F32 = jnp.float32
BF16 = jnp.bfloat16

D_MODEL = 1024
DEPTH = 4
PAGE = 128
POOL_DIM = 512
POOL_WINDOWS = (2, 4, 8, 16)
POOL_GD = 128
POOL_PAD = 16
POOL_STATE = 15
MLA_HEADS = 8
MLA_NOPE = 64
MLA_ROPE = 32
MLA_V = 64
MLA_Q_RANK = 256
MLA_KV_RANK = 128
MLA_SCALE = (MLA_NOPE + MLA_ROPE) ** -0.5
ROPE_THETA = 10000.0
DIFF_HEADS = 8
DIFF_HD = 64
DIFF_KVH = 2
DIFF_GROUP = 4
DIFF_SCALE = DIFF_HD ** -0.5
D_FF = 4096
ALPHA = (2 * DEPTH) ** 0.25
EPS = 1e-5
QCAT = 256
EVEN_IN_P = 1152

LANE = 128
TOKEN_TILE = 512
FLASH_TQ = 256
FF_CHUNK = 1024
PAGES_PER_STEP = 16
NEG_INF = float("-inf")


def _cparams(sem, vmem_mib):
    return pltpu.CompilerParams(dimension_semantics=sem, vmem_limit_bytes=vmem_mib * 1024 * 1024)


def _const_spec(shape):
    nd = len(shape)
    return pl.BlockSpec(shape, lambda *_: (0,) * nd, pipeline_mode=pl.Buffered(1))


def _dot(a, b):
    return jnp.dot(a, b, preferred_element_type=F32)


def _dot_nt(a, b):
    return lax.dot_general(a, b, (((1,), (1,)), ((), ())), preferred_element_type=F32)


def _rms(x, g):
    return x * lax.rsqrt(jnp.mean(x * x, -1, keepdims=True) + EPS) * g


def _ln(x, g, b):
    mu = jnp.mean(x, -1, keepdims=True)
    xc = x - mu
    var = jnp.mean(xc * xc, -1, keepdims=True)
    return xc * lax.rsqrt(var + EPS) * g + b


def _even_in_kernel(x_ref, w_in_ref, qn_ref, wuq_ref, kvn_ref, wbd_ref, place_ref,
                    cq_ref, sq_ref, ck_ref, sk_ref,
                    u_ref, qcat_ref, kcat_ref, ckv_ref, kr_ref):
    h = _dot(x_ref[...].astype(BF16), w_in_ref[...])
    u_ref[...] = h[:, :POOL_DIM]
    cqn = _rms(h[:, 512:768], qn_ref[...])
    q = _dot(cqn.astype(BF16), wuq_ref[...])
    rot = q[:, 512:768] * cq_ref[...] + q[:, 768:1024] * sq_ref[...]
    qlat = _dot(q[:, :512].astype(BF16), wbd_ref[...])
    qrope = _dot(rot.astype(BF16), place_ref[...])
    for hh in range(MLA_HEADS):
        qcat_ref[:, hh * QCAT:hh * QCAT + LANE] = qlat[:, hh * LANE:(hh + 1) * LANE].astype(BF16)
        qcat_ref[:, hh * QCAT + LANE:(hh + 1) * QCAT] = qrope[:, hh * LANE:(hh + 1) * LANE].astype(BF16)
    ckvn = _rms(h[:, 768:896], kvn_ref[...])
    ckv_ref[...] = ckvn
    krr = h[:, 896:1024] * ck_ref[...] + h[:, 1024:1152] * sk_ref[...]
    kr_ref[...] = krr[:, :MLA_ROPE]
    kcat_ref[:, :LANE] = ckvn.astype(BF16)
    kcat_ref[:, LANE:] = krr.astype(BF16)


def _even_in(x2d, wts, tabs, n_tab_blocks, tm):
    m = x2d.shape[0]
    row = lambda w: pl.BlockSpec((tm, w), lambda i: (i, 0))
    tab = lambda w: pl.BlockSpec((tm, w), lambda i: (i % n_tab_blocks, 0))
    return pl.pallas_call(
        _even_in_kernel,
        grid=(m // tm,),
        in_specs=[row(D_MODEL), _const_spec((D_MODEL, EVEN_IN_P)), _const_spec((1, MLA_Q_RANK)),
                  _const_spec((MLA_Q_RANK, 1024)), _const_spec((1, MLA_KV_RANK)),
                  _const_spec((512, 1024)), _const_spec((256, 1024)),
                  tab(256), tab(256), tab(LANE), tab(LANE)],
        out_specs=[row(POOL_DIM), row(MLA_HEADS * QCAT), row(QCAT), row(MLA_KV_RANK), row(MLA_ROPE)],
        out_shape=[jax.ShapeDtypeStruct((m, POOL_DIM), F32),
                   jax.ShapeDtypeStruct((m, MLA_HEADS * QCAT), BF16),
                   jax.ShapeDtypeStruct((m, QCAT), BF16),
                   jax.ShapeDtypeStruct((m, MLA_KV_RANK), F32),
                   jax.ShapeDtypeStruct((m, MLA_ROPE), F32)],
        compiler_params=_cparams(("arbitrary",), 40),
        name="even_in",
    )(x2d, wts["w_in"], wts["q_norm"], wts["w_uq"], wts["kv_norm"], wts["w_bd"], wts["place"], *tabs)


def _pool_kernel(u_ref, w_ref, sc_ref, o_ref, ext_ref, *, start_pos):
    bb, seq, _ = u_ref.shape
    ext_ref[:, :POOL_PAD, :] = jnp.zeros((bb, POOL_PAD, POOL_DIM), F32)
    ext_ref[:, POOL_PAD:, :] = u_ref[...]
    pos = start_pos + lax.broadcasted_iota(jnp.int32, (1, seq, 1), 1).astype(F32)
    for g, win in enumerate(POOL_WINDOWS):
        lanes = slice(g * POOL_GD, (g + 1) * POOL_GD)
        tot = ext_ref[:, POOL_PAD:POOL_PAD + seq, lanes]
        for j in range(1, win):
            tot = tot + ext_ref[:, POOL_PAD - j:POOL_PAD - j + seq, lanes]
        cnt = jnp.minimum(pos + 1.0, float(win))
        d = tot / cnt - u_ref[:, :, lanes]
        y = _dot(d.reshape(bb * seq, POOL_GD).astype(BF16), w_ref[g])
        o_ref[:, :, lanes] = (y.reshape(bb, seq, POOL_GD) * sc_ref[:, lanes]).astype(BF16)


def _pool(u_ext, pool_w, pool_scale, start_pos, bb):
    nb, seq, _ = u_ext.shape
    blk = pl.BlockSpec((bb, seq, POOL_DIM), lambda i: (i, 0, 0))
    return pl.pallas_call(
        functools.partial(_pool_kernel, start_pos=float(start_pos)),
        grid=(nb // bb,),
        in_specs=[blk, _const_spec((len(POOL_WINDOWS), POOL_GD, POOL_GD)), _const_spec((1, POOL_DIM))],
        out_specs=blk,
        out_shape=jax.ShapeDtypeStruct((nb, seq, POOL_DIM), BF16),
        scratch_shapes=[pltpu.VMEM((bb, POOL_PAD + seq, POOL_DIM), F32)],
        compiler_params=_cparams(("arbitrary",), 40),
        name="pool_mix",
    )(u_ext, pool_w, pool_scale)


def _flash_sweep(qs_ref, k_ref, v_ref, m_ref, l_ref, acc_ref, *, q_idx, tq, groups, scale):
    rows = groups * tq
    m_ref[...] = jnp.full((rows, 1), NEG_INF, F32)
    l_ref[...] = jnp.zeros((rows, 1), F32)
    acc_ref[...] = jnp.zeros((rows, LANE), F32)
    q = qs_ref[...]

    def update(j, diagonal):
        start = pl.multiple_of(j * tq, tq)
        s = _dot_nt(q, k_ref[pl.ds(start, tq), :])
        if scale is not None:
            s = s * scale
        if diagonal:
            keep = (lax.broadcasted_iota(jnp.int32, (1, tq, tq), 2)
                    <= lax.broadcasted_iota(jnp.int32, (1, tq, tq), 1))
            s = jnp.where(keep, s.reshape(groups, tq, tq), NEG_INF).reshape(rows, tq)
        m_old = m_ref[...]
        m_new = jnp.maximum(m_old, jnp.max(s, -1, keepdims=True))
        p = jnp.exp(s - m_new)
        alpha = jnp.exp(m_old - m_new)
        l_ref[...] = alpha * l_ref[...] + jnp.sum(p, -1, keepdims=True)
        acc_ref[...] = alpha * acc_ref[...] + _dot(p.astype(BF16), v_ref[pl.ds(start, tq), :])
        m_ref[...] = m_new

    def body(j, carry):
        update(j, False)
        return carry

    lax.fori_loop(0, q_idx, body, 0)
    update(q_idx, True)


def _flash_mla_kernel(q_ref, k_ref, o_ref, qs_ref, m_ref, l_ref, acc_ref, *, tq):
    for hh in range(MLA_HEADS):
        qs_ref[hh * tq:(hh + 1) * tq, :] = q_ref[0, :, hh * QCAT:(hh + 1) * QCAT]
    kv = k_ref.at[0]
    _flash_sweep(qs_ref, kv, kv.at[:, :LANE], m_ref, l_ref, acc_ref,
                 q_idx=pl.program_id(1), tq=tq, groups=MLA_HEADS, scale=MLA_SCALE)
    for hh in range(MLA_HEADS):
        rows = slice(hh * tq, (hh + 1) * tq)
        o_ref[0, :, hh * LANE:(hh + 1) * LANE] = (acc_ref[rows, :] / l_ref[rows, :]).astype(BF16)


def _flash_mla(qcat, kcat, tq):
    b, s, _ = qcat.shape
    rows = MLA_HEADS * tq
    return pl.pallas_call(
        functools.partial(_flash_mla_kernel, tq=tq),
        grid=(b, s // tq),
        in_specs=[pl.BlockSpec((1, tq, MLA_HEADS * QCAT), lambda bi, i: (bi, i, 0)),
                  pl.BlockSpec((1, s, QCAT), lambda bi, i: (bi, 0, 0))],
        out_specs=pl.BlockSpec((1, tq, MLA_HEADS * LANE), lambda bi, i: (bi, i, 0)),
        out_shape=jax.ShapeDtypeStruct((b, s, MLA_HEADS * LANE), BF16),
        scratch_shapes=[pltpu.VMEM((rows, QCAT), BF16), pltpu.VMEM((rows, 1), F32),
                        pltpu.VMEM((rows, 1), F32), pltpu.VMEM((rows, LANE), F32)],
        compiler_params=_cparams(("arbitrary", "arbitrary"), 48),
        name="flash_mla",
    )(qcat, kcat)


def _diff_lambda(lq1_ref, lk1_ref, lq2_ref, lk2_ref, lam_init):
    a = jnp.sum(lq1_ref[...] * lk1_ref[...], -1, keepdims=True)
    b = jnp.sum(lq2_ref[...] * lk2_ref[...], -1, keepdims=True)
    return jnp.exp(a) - jnp.exp(b) + lam_init


def _diff_mask_rows(tile):
    lane = lax.broadcasted_iota(jnp.int32, tile.shape, 1)
    scaled = tile * jnp.asarray(DIFF_SCALE, tile.dtype)
    zero = jnp.zeros_like(scaled)
    return jnp.where(lane < DIFF_HD, scaled, zero), jnp.where(lane >= DIFF_HD, scaled, zero)


def _diff_finish(o1, o2, lam, norm_g, lam_init):
    o = o1 - lam * o2
    return _rms(o, norm_g) * (1.0 - lam_init)


def _flash_diff_kernel(q_ref, k_ref, v_ref, lq1_ref, lk1_ref, lq2_ref, lk2_ref, ng_ref,
                       o_ref, qs_ref, m_ref, l_ref, acc_ref, *, tq, lam_init):
    for g in range(DIFF_GROUP):
        qa, qb = _diff_mask_rows(q_ref[0, :, g * LANE:(g + 1) * LANE])
        qs_ref[(2 * g) * tq:(2 * g + 1) * tq, :] = qa
        qs_ref[(2 * g + 1) * tq:(2 * g + 2) * tq, :] = qb
    _flash_sweep(qs_ref, k_ref.at[0], v_ref.at[0], m_ref, l_ref, acc_ref,
                 q_idx=pl.program_id(2), tq=tq, groups=2 * DIFF_GROUP, scale=None)
    lam = _diff_lambda(lq1_ref, lk1_ref, lq2_ref, lk2_ref, lam_init)
    for g in range(DIFF_GROUP):
        r1 = slice((2 * g) * tq, (2 * g + 1) * tq)
        r2 = slice((2 * g + 1) * tq, (2 * g + 2) * tq)
        o = _diff_finish(acc_ref[r1, :] / l_ref[r1, :], acc_ref[r2, :] / l_ref[r2, :],
                         lam, ng_ref[...], lam_init)
        o_ref[0, :, g * LANE:(g + 1) * LANE] = o.astype(BF16)


def _flash_diff(q, k, v, lam_vecs, norm_g, lam_init, tq):
    b, s, _ = q.shape
    rows = 2 * DIFF_GROUP * tq
    gw = DIFF_GROUP * LANE
    vec = _const_spec((1, DIFF_HD))
    return pl.pallas_call(
        functools.partial(_flash_diff_kernel, tq=tq, lam_init=lam_init),
        grid=(b, DIFF_KVH, s // tq),
        in_specs=[pl.BlockSpec((1, tq, gw), lambda bi, kh, i: (bi, i, kh)),
                  pl.BlockSpec((1, s, LANE), lambda bi, kh, i: (bi, 0, kh)),
                  pl.BlockSpec((1, s, LANE), lambda bi, kh, i: (bi, 0, kh)),
                  vec, vec, vec, vec, _const_spec((1, LANE))],
        out_specs=pl.BlockSpec((1, tq, gw), lambda bi, kh, i: (bi, i, kh)),
        out_shape=jax.ShapeDtypeStruct((b, s, DIFF_KVH * gw), BF16),
        scratch_shapes=[pltpu.VMEM((rows, LANE), BF16), pltpu.VMEM((rows, 1), F32),
                        pltpu.VMEM((rows, 1), F32), pltpu.VMEM((rows, LANE), F32)],
        compiler_params=_cparams(("arbitrary", "arbitrary", "arbitrary"), 48),
        name="flash_diff",
    )(q, k, v, *lam_vecs, norm_g)


def _online_update(s, vals, m_ref, l_ref, acc_ref):
    m_old = m_ref[...]
    m_new = jnp.maximum(m_old, jnp.max(s, -1, keepdims=True))
    p = jnp.exp(s - m_new)
    alpha = jnp.exp(m_old - m_new)
    l_ref[...] = alpha * l_ref[...] + jnp.sum(p, -1, keepdims=True)
    pv = None
    for j, vj in enumerate(vals):
        t = _dot(p[:, j * PAGE:(j + 1) * PAGE].astype(BF16), vj)
        pv = t if pv is None else pv + t
    acc_ref[...] = alpha * acc_ref[...] + pv
    m_ref[...] = m_new


def _new_token_mask(s, t_len):
    r, n = s.shape
    t = lax.broadcasted_iota(jnp.int32, (r, n), 0) % t_len
    j = lax.broadcasted_iota(jnp.int32, (r, n), 1)
    return jnp.where(j <= t, s, NEG_INF)


def _decode_mla_kernel(pt_ref, q_ref, knew_ref, *rest, pps, t_len):
    ckv_refs, kr_refs = rest[:pps], rest[pps:2 * pps]
    o_ref, m_ref, l_ref, acc_ref = rest[2 * pps:]
    c = pl.program_id(1)

    @pl.when(c == 0)
    def _():
        m_ref[...] = jnp.full(m_ref.shape, NEG_INF, F32)
        l_ref[...] = jnp.zeros(l_ref.shape, F32)
        acc_ref[...] = jnp.zeros(acc_ref.shape, F32)

    q = q_ref[0]
    q_lat, q_rope = q[:, :LANE], q[:, LANE:LANE + MLA_ROPE]
    vals, scores = [], []
    for j in range(pps):
        ckv = ckv_refs[j][...].astype(BF16)
        vals.append(ckv)
        scores.append(_dot_nt(q_lat, ckv) + _dot_nt(q_rope, kr_refs[j][...].astype(BF16)))
    _online_update(jnp.concatenate(scores, -1) * MLA_SCALE, vals, m_ref, l_ref, acc_ref)

    @pl.when(c == pl.num_programs(1) - 1)
    def _():
        kn = knew_ref[0]
        s = _new_token_mask(_dot_nt(q, kn) * MLA_SCALE, t_len)
        _online_update(s, [kn[:, :LANE]], m_ref, l_ref, acc_ref)
        o_ref[0] = (acc_ref[...] / l_ref[...]).astype(BF16)


def _decode_mla(page_table, q_rows, k_new, cache_ckv, cache_kr, layer, t_len, pps):
    db, rows, _ = q_rows.shape
    n_pages = page_table.shape[1]
    pt = page_table.reshape(-1)

    def page(width, j):
        return pl.BlockSpec((None, None, PAGE, width),
                            lambda b, c, pt_ref: (layer, pt_ref[b * n_pages + c * pps + j], 0, 0))

    grid_spec = pltpu.PrefetchScalarGridSpec(
        num_scalar_prefetch=1,
        grid=(db, n_pages // pps),
        in_specs=[pl.BlockSpec((1, rows, QCAT), lambda b, c, pt_ref: (b, 0, 0)),
                  pl.BlockSpec((1, PAGE, QCAT), lambda b, c, pt_ref: (b, 0, 0))]
                 + [page(MLA_KV_RANK, j) for j in range(pps)]
                 + [page(MLA_ROPE, j) for j in range(pps)],
        out_specs=pl.BlockSpec((1, rows, LANE), lambda b, c, pt_ref: (b, 0, 0)),
        scratch_shapes=[pltpu.VMEM((rows, 1), F32), pltpu.VMEM((rows, 1), F32),
                        pltpu.VMEM((rows, LANE), F32)],
    )
    return pl.pallas_call(
        functools.partial(_decode_mla_kernel, pps=pps, t_len=t_len),
        grid_spec=grid_spec,
        out_shape=jax.ShapeDtypeStruct((db, rows, LANE), BF16),
        compiler_params=_cparams(("arbitrary", "arbitrary"), 32),
        name="decode_mla",
    )(pt, q_rows, k_new, *([cache_ckv] * pps), *([cache_kr] * pps))


def _decode_diff_kernel(pt_ref, q_ref, knew_ref, vnew_ref, lq1_ref, lk1_ref, lq2_ref, lk2_ref, ng_ref,
                        *rest, pps, t_len, lam_init):
    k_refs, v_refs = rest[:pps], rest[pps:2 * pps]
    o_ref, qs_ref, m_ref, l_ref, acc_ref = rest[2 * pps:]
    c = pl.program_id(1)
    half = DIFF_GROUP * t_len

    @pl.when(c == 0)
    def _():
        m_ref[...] = jnp.full(m_ref.shape, NEG_INF, F32)
        l_ref[...] = jnp.zeros(l_ref.shape, F32)
        acc_ref[...] = jnp.zeros(acc_ref.shape, F32)
        for kh in range(DIFF_KVH):
            qa, qb = _diff_mask_rows(q_ref[0, kh * half:(kh + 1) * half, :])
            qs_ref[kh, :half, :] = qa
            qs_ref[kh, half:, :] = qb

    def update(k_tiles, v_tiles, masked):
        for kh in range(DIFF_KVH):
            lanes = slice(kh * LANE, (kh + 1) * LANE)
            q = qs_ref[kh]
            s = jnp.concatenate([_dot_nt(q, kt[:, lanes]) for kt in k_tiles], -1)
            if masked:
                s = _new_token_mask(s, t_len)
            _online_update(s, [vt[:, lanes] for vt in v_tiles], m_ref.at[kh], l_ref.at[kh], acc_ref.at[kh])

    update([r[...].astype(BF16) for r in k_refs], [r[...].astype(BF16) for r in v_refs], False)

    @pl.when(c == pl.num_programs(1) - 1)
    def _():
        update([knew_ref[0]], [vnew_ref[0]], True)
        lam = _diff_lambda(lq1_ref, lk1_ref, lq2_ref, lk2_ref, lam_init)
        for kh in range(DIFF_KVH):
            o1 = acc_ref[kh, :half, :] / l_ref[kh, :half, :]
            o2 = acc_ref[kh, half:, :] / l_ref[kh, half:, :]
            o = _diff_finish(o1, o2, lam, ng_ref[...], lam_init)
            o_ref[0, kh * half:(kh + 1) * half, :] = o.astype(BF16)


def _decode_diff(page_table, q_rows, k_new, v_new, lam_vecs, norm_g, cache_k, cache_v,
                 layer, t_len, lam_init, pps):
    db, rows, _ = q_rows.shape
    n_pages = page_table.shape[1]
    pt = page_table.reshape(-1)
    kvw = DIFF_KVH * LANE
    half = DIFF_GROUP * t_len

    def page(j):
        return pl.BlockSpec((None, None, PAGE, kvw),
                            lambda b, c, pt_ref: (layer, pt_ref[b * n_pages + c * pps + j], 0, 0))

    const = lambda shape: pl.BlockSpec(shape, lambda b, c, pt_ref: (0,) * len(shape))
    per_b = lambda shape: pl.BlockSpec(shape, lambda b, c, pt_ref: (b, 0, 0))
    grid_spec = pltpu.PrefetchScalarGridSpec(
        num_scalar_prefetch=1,
        grid=(db, n_pages // pps),
        in_specs=[per_b((1, rows, LANE)), per_b((1, PAGE, kvw)), per_b((1, PAGE, kvw)),
                  const((1, DIFF_HD)), const((1, DIFF_HD)), const((1, DIFF_HD)), const((1, DIFF_HD)),
                  const((1, LANE))]
                 + [page(j) for j in range(pps)] + [page(j) for j in range(pps)],
        out_specs=per_b((1, rows, LANE)),
        scratch_shapes=[pltpu.VMEM((DIFF_KVH, 2 * half, LANE), BF16),
                        pltpu.VMEM((DIFF_KVH, 2 * half, 1), F32),
                        pltpu.VMEM((DIFF_KVH, 2 * half, 1), F32),
                        pltpu.VMEM((DIFF_KVH, 2 * half, LANE), F32)],
    )
    return pl.pallas_call(
        functools.partial(_decode_diff_kernel, pps=pps, t_len=t_len, lam_init=lam_init),
        grid_spec=grid_spec,
        out_shape=jax.ShapeDtypeStruct((db, rows, LANE), BF16),
        compiler_params=_cparams(("arbitrary", "arbitrary"), 40),
        name="decode_diff",
    )(pt, q_rows, k_new, v_new, *lam_vecs, norm_g, *([cache_k] * pps), *([cache_v] * pps))


def _odd_in_kernel(x_ref, w_ref, q_ref, k_ref, v_ref, kb_ref, vb_ref):
    h = _dot(x_ref[...].astype(BF16), w_ref[...])
    nq = DIFF_HEADS * 2 * DIFF_HD
    nkv = DIFF_KVH * 2 * DIFF_HD
    q_ref[...] = h[:, :nq].astype(BF16)
    k = h[:, nq:nq + nkv]
    v = h[:, nq + nkv:]
    k_ref[...] = k
    v_ref[...] = v
    kb_ref[...] = k.astype(BF16)
    vb_ref[...] = v.astype(BF16)


def _odd_in(x2d, w_in, tm):
    m = x2d.shape[0]
    nq = DIFF_HEADS * 2 * DIFF_HD
    nkv = DIFF_KVH * 2 * DIFF_HD
    row = lambda w: pl.BlockSpec((tm, w), lambda i: (i, 0))
    return pl.pallas_call(
        _odd_in_kernel,
        grid=(m // tm,),
        in_specs=[row(D_MODEL), _const_spec((D_MODEL, nq + 2 * nkv))],
        out_specs=[row(nq), row(nkv), row(nkv), row(nkv), row(nkv)],
        out_shape=[jax.ShapeDtypeStruct((m, nq), BF16),
                   jax.ShapeDtypeStruct((m, nkv), F32), jax.ShapeDtypeStruct((m, nkv), F32),
                   jax.ShapeDtypeStruct((m, nkv), BF16), jax.ShapeDtypeStruct((m, nkv), BF16)],
        compiler_params=_cparams(("arbitrary",), 40),
        name="odd_in",
    )(x2d, w_in)


def _post_kernel(*refs, even):
    if even:
        (x_ref, a_ref, pool_ref, wuv_ref, wmix_ref, g1_ref, b1_ref, g2_ref, b2_ref,
         w1_ref, fb1_ref, w2_ref, fb2_ref, o_ref) = refs
        o = _dot(a_ref[...], wuv_ref[...]).astype(BF16)
        mix = _dot(pool_ref[...], wmix_ref[:POOL_DIM, :]) + _dot(o, wmix_ref[POOL_DIM:, :])
    else:
        (x_ref, a_ref, wmix_ref, g1_ref, b1_ref, g2_ref, b2_ref,
         w1_ref, fb1_ref, w2_ref, fb2_ref, o_ref) = refs
        mix = _dot(a_ref[...], wmix_ref[...])
    x1 = _ln(ALPHA * x_ref[...] + mix, g1_ref[...], b1_ref[...])
    x1b = x1.astype(BF16)
    ffn = None
    for c in range(D_FF // FF_CHUNK):
        cols = slice(c * FF_CHUNK, (c + 1) * FF_CHUNK)
        hdn = jnp.square(jnp.maximum(_dot(x1b, w1_ref[:, cols]) + fb1_ref[:, cols], 0.0))
        part = _dot(hdn.astype(BF16), w2_ref[cols, :])
        ffn = part if ffn is None else ffn + part
    o_ref[...] = _ln(ALPHA * x1 + (ffn + fb2_ref[...]), g2_ref[...], b2_ref[...])


def _post(x2d, a, extra, wts, even, tm):
    m = x2d.shape[0]
    row = lambda w: pl.BlockSpec((tm, w), lambda i: (i, 0))
    vec = _const_spec((1, D_MODEL))
    ins = [x2d, a]
    specs = [row(D_MODEL), row(a.shape[1])]
    if even:
        ins += [extra, wts["w_uv"]]
        specs += [row(POOL_DIM), _const_spec((MLA_HEADS * LANE, MLA_HEADS * MLA_V))]
    ins += [wts["w_mix"], wts["g1"], wts["b1"], wts["g2"], wts["b2"],
            wts["w1"], wts["fb1"], wts["w2"], wts["fb2"]]
    specs += [_const_spec((D_MODEL, D_MODEL)), vec, vec, vec, vec,
              _const_spec((D_MODEL, D_FF)), _const_spec((1, D_FF)), _const_spec((D_FF, D_MODEL)), vec]
    return pl.pallas_call(
        functools.partial(_post_kernel, even=even),
        grid=(m // tm,),
        in_specs=specs,
        out_specs=row(D_MODEL),
        out_shape=jax.ShapeDtypeStruct((m, D_MODEL), F32),
        compiler_params=_cparams(("arbitrary",), 56),
        name="post_even" if even else "post_odd",
    )(*ins)


def _rope_tables(pos):
    half = MLA_ROPE // 2
    freq = ROPE_THETA ** (-jnp.arange(half, dtype=F32) / half)
    ang = pos[:, None] * freq[None, :]
    cos, sin = jnp.cos(ang), jnp.sin(ang)
    cc = jnp.concatenate([cos, cos], -1)
    ss = jnp.concatenate([-sin, sin], -1)
    pad = jnp.zeros((pos.shape[0], LANE - MLA_ROPE), F32)
    return (jnp.tile(cc, (1, MLA_HEADS)), jnp.tile(ss, (1, MLA_HEADS)),
            jnp.concatenate([cc, pad], -1), jnp.concatenate([ss, pad], -1))


def _block_diag(blocks):
    h, r, c = blocks.shape
    eye = jnp.eye(h, dtype=blocks.dtype)
    return (blocks[:, :, None, :] * eye[:, None, :, None]).reshape(h * r, h * c)


def _even_weights(w_in, q_norm, w_uq, kv_norm, w_uk, w_uv, w_out):
    half = MLA_ROPE // 2
    kr0 = POOL_DIM + MLA_Q_RANK + MLA_KV_RANK
    zpad = jnp.zeros((D_MODEL, LANE - MLA_ROPE), F32)
    w_in_p = jnp.concatenate(
        [w_in, zpad, w_in[:, kr0 + half:], w_in[:, kr0:kr0 + half], zpad], -1).astype(BF16)
    uq = w_uq.reshape(MLA_Q_RANK, MLA_HEADS, MLA_NOPE + MLA_ROPE)
    nope = uq[:, :, :MLA_NOPE].reshape(MLA_Q_RANK, MLA_HEADS * MLA_NOPE)
    rope = uq[:, :, MLA_NOPE:]
    rope_a = rope.reshape(MLA_Q_RANK, MLA_HEADS * MLA_ROPE)
    rope_b = jnp.concatenate([rope[:, :, half:], rope[:, :, :half]], -1).reshape(MLA_Q_RANK, MLA_HEADS * MLA_ROPE)
    place = jnp.concatenate([jnp.eye(MLA_ROPE, dtype=F32), jnp.zeros((MLA_ROPE, LANE - MLA_ROPE), F32)], -1)
    return {
        "w_in": w_in_p,
        "q_norm": q_norm.reshape(1, -1),
        "w_uq": jnp.concatenate([nope, rope_a, rope_b], -1).astype(BF16),
        "kv_norm": kv_norm.reshape(1, -1),
        "w_bd": _block_diag(jnp.transpose(w_uk, (1, 2, 0))).astype(BF16),
        "place": _block_diag(jnp.broadcast_to(place, (MLA_HEADS,) + place.shape)).astype(BF16),
        "w_uv": _block_diag(jnp.transpose(w_uv, (1, 0, 2))).astype(BF16),
        "w_mix": w_out.astype(BF16),
    }


def _ffn_weights(g1, b1, g2, b2, w1, fb1, w2, fb2):
    r = lambda v: v.reshape(1, -1)
    return {"g1": r(g1), "b1": r(b1), "g2": r(g2), "b2": r(b2),
            "w1": w1.astype(BF16), "fb1": r(fb1), "w2": w2.astype(BF16), "fb2": r(fb2)}


def _token_tile(m):
    return TOKEN_TILE if m % TOKEN_TILE == 0 else m


def kernel(x_prompt, x_sample, cache_mla_ckv, cache_mla_krope, cache_diff_k, cache_diff_v, state_pool,
           page_table, w_in_even, pool_w, pool_scale, mla_q_norm, mla_w_uq, mla_kv_norm, mla_w_uk,
           mla_w_uv, w_out_even, w_in_odd, diff_lq1, diff_lk1, diff_lq2, diff_lk2, diff_norm, w_out_odd,
           ln1_g, ln1_b, ln2_g, ln2_b, mlp_w1, mlp_b1, mlp_w2, mlp_b2):
    bsz, s_len, d = x_prompt.shape
    db, t_len, _ = x_sample.shape
    n_pages = page_table.shape[1]
    past = n_pages * PAGE
    mp, ms = bsz * s_len, db * t_len
    tmp, tms = _token_tile(mp), _token_tile(ms)
    tq = min(FLASH_TQ, s_len)
    pps = min(PAGES_PER_STEP, n_pages)
    n_pool = cache_diff_k.shape[1]
    kvw = DIFF_KVH * 2 * DIFF_HD
    cache_dk = cache_diff_k.reshape(cache_diff_k.shape[0], n_pool, PAGE, kvw)
    cache_dv = cache_diff_v.reshape(cache_diff_v.shape[0], n_pool, PAGE, kvw)

    tabs_p = _rope_tables(jnp.arange(s_len, dtype=F32))
    tabs_s = tuple(jnp.tile(t, (tms // t_len, 1)) for t in _rope_tables(past + jnp.arange(t_len, dtype=F32)))
    pad_new = lambda a: jnp.pad(a, ((0, 0), (0, PAGE - t_len), (0, 0)))
    hist = 24 - t_len

    xp = x_prompt.reshape(mp, d)
    xs = x_sample.reshape(ms, d)
    outs = {k: [] for k in ("p_ckv", "p_kr", "p_dk", "p_dv", "p_pool", "s_ckv", "s_kr", "s_dk", "s_dv", "s_pool")}
    for i in range(DEPTH):
        ffn = _ffn_weights(ln1_g[i], ln1_b[i], ln2_g[i], ln2_b[i], mlp_w1[i], mlp_b1[i], mlp_w2[i], mlp_b2[i])
        if i % 2 == 0:
            e = i // 2
            wts = _even_weights(w_in_even[e], mla_q_norm[e], mla_w_uq[e], mla_kv_norm[e], mla_w_uk[e],
                                mla_w_uv[e], w_out_even[e])
            wts.update(ffn)
            pw = pool_w[e].astype(BF16)
            psc = pool_scale[e].reshape(1, -1)
            u, qcat, kcat, ckv, kr = _even_in(xp, wts, tabs_p, s_len // tmp if s_len % tmp == 0 else 1, tmp)
            u3 = u.reshape(bsz, s_len, POOL_DIM)
            pool_o = _pool(u3, pw, psc, 0, 1).reshape(mp, POOL_DIM)
            o_lat = _flash_mla(qcat.reshape(bsz, s_len, -1), kcat.reshape(bsz, s_len, QCAT), tq)
            xp = _post(xp, o_lat.reshape(mp, -1), pool_o, wts, True, tmp)
            outs["p_ckv"].append(ckv.reshape(bsz, s_len, -1))
            outs["p_kr"].append(kr.reshape(bsz, s_len, -1))
            outs["p_pool"].append(u3[:, s_len - POOL_STATE:])
            u, qcat, kcat, ckv, kr = _even_in(xs, wts, tabs_s, 1, tms)
            u3 = u.reshape(db, t_len, POOL_DIM)
            u_ext = jnp.concatenate([state_pool[e], u3], axis=1)
            u_pad = jnp.pad(u_ext, ((0, 0), (24 - u_ext.shape[1], 0), (0, 0)))
            bb = db
            while bb * u_pad.shape[1] > 2048 and bb % 2 == 0:
                bb //= 2
            pool_o = _pool(u_pad, pw, psc, past - hist, bb)[:, hist:].reshape(ms, POOL_DIM)
            q_rows = jnp.transpose(qcat.reshape(db, t_len, MLA_HEADS, QCAT), (0, 2, 1, 3))
            o_rows = _decode_mla(page_table, q_rows.reshape(db, MLA_HEADS * t_len, QCAT),
                                 pad_new(kcat.reshape(db, t_len, QCAT)),
                                 cache_mla_ckv, cache_mla_krope, e, t_len, pps)
            o_lat = jnp.transpose(o_rows.reshape(db, MLA_HEADS, t_len, LANE), (0, 2, 1, 3)).reshape(ms, -1)
            xs = _post(xs, o_lat, pool_o, wts, True, tms)
            outs["s_ckv"].append(ckv.reshape(db, t_len, -1))
            outs["s_kr"].append(kr.reshape(db, t_len, -1))
            outs["s_pool"].append(u_ext[:, u_ext.shape[1] - POOL_STATE:])
        else:
            o = i // 2
            lam_init = 0.8 - 0.6 * math.exp(-0.3 * i)
            lam_vecs = tuple(v[o].reshape(1, -1) for v in (diff_lq1, diff_lk1, diff_lq2, diff_lk2))
            norm_g = diff_norm[o].reshape(1, -1)
            wts = dict(ffn)
            wts["w_mix"] = w_out_odd[o].astype(BF16)
            w_in = w_in_odd[o].astype(BF16)
            q, k, v, kb, vb = _odd_in(xp, w_in, tmp)
            att = _flash_diff(q.reshape(bsz, s_len, -1), kb.reshape(bsz, s_len, kvw),
                              vb.reshape(bsz, s_len, kvw), lam_vecs, norm_g, lam_init, tq)
            xp = _post(xp, att.reshape(mp, -1), None, wts, False, tmp)
            outs["p_dk"].append(k.reshape(bsz, s_len, DIFF_KVH, 2 * DIFF_HD))
            outs["p_dv"].append(v.reshape(bsz, s_len, DIFF_KVH, 2 * DIFF_HD))
            q, k, v, kb, vb = _odd_in(xs, w_in, tms)
            heads = DIFF_KVH * DIFF_GROUP
            q_rows = jnp.transpose(q.reshape(db, t_len, heads, LANE), (0, 2, 1, 3)).reshape(db, heads * t_len, LANE)
            o_rows = _decode_diff(page_table, q_rows, pad_new(kb.reshape(db, t_len, kvw)),
                                  pad_new(vb.reshape(db, t_len, kvw)), lam_vecs, norm_g,
                                  cache_dk, cache_dv, o, t_len, lam_init, pps)
            att = jnp.transpose(o_rows.reshape(db, heads, t_len, LANE), (0, 2, 1, 3)).reshape(ms, -1)
            xs = _post(xs, att, None, wts, False, tms)
            outs["s_dk"].append(k.reshape(db, t_len, DIFF_KVH, 2 * DIFF_HD))
            outs["s_dv"].append(v.reshape(db, t_len, DIFF_KVH, 2 * DIFF_HD))

    st = lambda name: jnp.stack(outs[name])
    return (xp.reshape(bsz, s_len, d), xs.reshape(db, t_len, d),
            st("p_ckv"), st("p_kr"), st("p_dk"), st("p_dv"), st("p_pool"),
            st("s_ckv"), st("s_kr"), st("s_dk"), st("s_dv"), st("s_pool"))
```

```python
import functools
import math

import jax
import jax.numpy as jnp
from jax import lax
from jax.experimental import pallas as pl
from jax.experimental.pallas import tpu as pltpu

F32 = jnp.float32
BF16 = jnp.bfloat16

D_MODEL = 1024
DEPTH = 4
PAGE = 128
POOL_DIM = 512
POOL_WINDOWS = (2, 4, 8, 16)
POOL_GD = 128
POOL_PAD = 16
POOL_STATE = 15
MLA_HEADS = 8
MLA_NOPE = 64
MLA_ROPE = 32
MLA_V = 64
MLA_Q_RANK = 256
MLA_KV_RANK = 128
MLA_SCALE = (MLA_NOPE + MLA_ROPE) ** -0.5
ROPE_THETA = 10000.0
DIFF_HEADS = 8
DIFF_HD = 64
DIFF_KVH = 2
DIFF_GROUP = 4
DIFF_SCALE = DIFF_HD ** -0.5
D_FF = 4096
ALPHA = (2 * DEPTH) ** 0.25
EPS = 1e-5
QCAT = 256
EVEN_IN_P = 1152

LANE = 128
TOKEN_TILE = 512
FLASH_TQ = 256
FF_CHUNK = 1024
PAGES_PER_STEP = 16
NEG_INF = float("-inf")


def _cparams(sem, vmem_mib):
    return pltpu.CompilerParams(dimension_semantics=sem, vmem_limit_bytes=vmem_mib * 1024 * 1024)


def _const_spec(shape):
    nd = len(shape)
    return pl.BlockSpec(shape, lambda *_: (0,) * nd, pipeline_mode=pl.Buffered(1))


def _dot(a, b):
    return jnp.dot(a, b, preferred_element_type=F32)


def _dot_nt(a, b):
    return lax.dot_general(a, b, (((1,), (1,)), ((), ())), preferred_element_type=F32)


def _rms(x, g):
    return x * lax.rsqrt(jnp.mean(x * x, -1, keepdims=True) + EPS) * g


def _ln(x, g, b):
    mu = jnp.mean(x, -1, keepdims=True)
    xc = x - mu
    var = jnp.mean(xc * xc, -1, keepdims=True)
    return xc * lax.rsqrt(var + EPS) * g + b


def _even_in_kernel(x_ref, w_in_ref, qn_ref, wuq_ref, kvn_ref, wbd_ref, place_ref,
                    cq_ref, sq_ref, ck_ref, sk_ref,
                    u_ref, qcat_ref, kcat_ref, ckv_ref, kr_ref):
    h = _dot(x_ref[...].astype(BF16), w_in_ref[...])
    u_ref[...] = h[:, :POOL_DIM]
    cqn = _rms(h[:, 512:768], qn_ref[...])
    q = _dot(cqn.astype(BF16), wuq_ref[...])
    rot = q[:, 512:768] * cq_ref[...] + q[:, 768:1024] * sq_ref[...]
    qlat = _dot(q[:, :512].astype(BF16), wbd_ref[...])
    qrope = _dot(rot.astype(BF16), place_ref[...])
    for hh in range(MLA_HEADS):
        qcat_ref[:, hh * QCAT:hh * QCAT + LANE] = qlat[:, hh * LANE:(hh + 1) * LANE].astype(BF16)
        qcat_ref[:, hh * QCAT + LANE:(hh + 1) * QCAT] = qrope[:, hh * LANE:(hh + 1) * LANE].astype(BF16)
    ckvn = _rms(h[:, 768:896], kvn_ref[...])
    ckv_ref[...] = ckvn
    krr = h[:, 896:1024] * ck_ref[...] + h[:, 1024:1152] * sk_ref[...]
    kr_ref[...] = krr[:, :MLA_ROPE]
    kcat_ref[:, :LANE] = ckvn.astype(BF16)
    kcat_ref[:, LANE:] = krr.astype(BF16)


def _even_in(x2d, wts, tabs, n_tab_blocks, tm):
    m = x2d.shape[0]
    row = lambda w: pl.BlockSpec((tm, w), lambda i: (i, 0))
    tab = lambda w: pl.BlockSpec((tm, w), lambda i: (i % n_tab_blocks, 0))
    return pl.pallas_call(
        _even_in_kernel,
        grid=(m // tm,),
        in_specs=[row(D_MODEL), _const_spec((D_MODEL, EVEN_IN_P)), _const_spec((1, MLA_Q_RANK)),
                  _const_spec((MLA_Q_RANK, 1024)), _const_spec((1, MLA_KV_RANK)),
                  _const_spec((512, 1024)), _const_spec((256, 1024)),
                  tab(256), tab(256), tab(LANE), tab(LANE)],
        out_specs=[row(POOL_DIM), row(MLA_HEADS * QCAT), row(QCAT), row(MLA_KV_RANK), row(MLA_ROPE)],
        out_shape=[jax.ShapeDtypeStruct((m, POOL_DIM), F32),
                   jax.ShapeDtypeStruct((m, MLA_HEADS * QCAT), BF16),
                   jax.ShapeDtypeStruct((m, QCAT), BF16),
                   jax.ShapeDtypeStruct((m, MLA_KV_RANK), F32),
                   jax.ShapeDtypeStruct((m, MLA_ROPE), F32)],
        compiler_params=_cparams(("arbitrary",), 40),
        name="even_in",
    )(x2d, wts["w_in"], wts["q_norm"], wts["w_uq"], wts["kv_norm"], wts["w_bd"], wts["place"], *tabs)


def _pool_kernel(u_ref, w_ref, sc_ref, o_ref, ext_ref, *, start_pos):
    bb, seq, _ = u_ref.shape
    ext_ref[:, :POOL_PAD, :] = jnp.zeros((bb, POOL_PAD, POOL_DIM), F32)
    ext_ref[:, POOL_PAD:, :] = u_ref[...]
    pos = start_pos + lax.broadcasted_iota(jnp.int32, (1, seq, 1), 1).astype(F32)
    for g, win in enumerate(POOL_WINDOWS):
        lanes = slice(g * POOL_GD, (g + 1) * POOL_GD)
        tot = ext_ref[:, POOL_PAD:POOL_PAD + seq, lanes]
        for j in range(1, win):
            tot = tot + ext_ref[:, POOL_PAD - j:POOL_PAD - j + seq, lanes]
        cnt = jnp.minimum(pos + 1.0, float(win))
        d = tot / cnt - u_ref[:, :, lanes]
        y = _dot(d.reshape(bb * seq, POOL_GD).astype(BF16), w_ref[g])
        o_ref[:, :, lanes] = (y.reshape(bb, seq, POOL_GD) * sc_ref[:, lanes]).astype(BF16)


def _pool(u_ext, pool_w, pool_scale, start_pos, bb):
    nb, seq, _ = u_ext.shape
    blk = pl.BlockSpec((bb, seq, POOL_DIM), lambda i: (i, 0, 0))
    return pl.pallas_call(
        functools.partial(_pool_kernel, start_pos=float(start_pos)),
        grid=(nb // bb,),
        in_specs=[blk, _const_spec((len(POOL_WINDOWS), POOL_GD, POOL_GD)), _const_spec((1, POOL_DIM))],
        out_specs=blk,
        out_shape=jax.ShapeDtypeStruct((nb, seq, POOL_DIM), BF16),
        scratch_shapes=[pltpu.VMEM((bb, POOL_PAD + seq, POOL_DIM), F32)],
        compiler_params=_cparams(("arbitrary",), 40),
        name="pool_mix",
    )(u_ext, pool_w, pool_scale)


def _flash_sweep(qs_ref, k_ref, vt_ref, m_ref, l_ref, acc_ref, *, q_idx, tq, groups, scale):
    rows = groups * tq
    m_ref[...] = jnp.full((1, rows), NEG_INF, F32)
    l_ref[...] = jnp.zeros((1, rows), F32)
    acc_ref[...] = jnp.zeros((LANE, rows), F32)
    q = qs_ref[...]

    def update(j, diagonal):
        start = pl.multiple_of(j * tq, tq)
        st = _dot_nt(k_ref[pl.ds(start, tq), :], q)
        if scale is not None:
            st = st * scale
        if diagonal:
            key = lax.broadcasted_iota(jnp.int32, (tq, rows), 0)
            t = lax.broadcasted_iota(jnp.int32, (tq, rows), 1) % tq
            st = jnp.where(key <= t, st, NEG_INF)
        m_old = m_ref[...]
        m_new = jnp.maximum(m_old, jnp.max(st, 0, keepdims=True))
        p = jnp.exp(st - m_new)
        alpha = jnp.exp(m_old - m_new)
        l_ref[...] = alpha * l_ref[...] + jnp.sum(p, 0, keepdims=True)
        acc_ref[...] = alpha * acc_ref[...] + _dot(vt_ref[j], p.astype(BF16))
        m_ref[...] = m_new

    def body(j, carry):
        update(j, False)
        return carry

    lax.fori_loop(0, q_idx, body, 0)
    update(q_idx, True)


def _flash_scratch(rows, dk):
    return [pltpu.VMEM((rows, dk), BF16), pltpu.VMEM((1, rows), F32),
            pltpu.VMEM((1, rows), F32), pltpu.VMEM((LANE, rows), F32)]


def _flash_mla_kernel(q_ref, k_ref, vt_ref, o_ref, qs_ref, m_ref, l_ref, acc_ref, *, tq):
    for hh in range(MLA_HEADS):
        qs_ref[hh * tq:(hh + 1) * tq, :] = q_ref[0, :, hh * QCAT:(hh + 1) * QCAT]
    _flash_sweep(qs_ref, k_ref.at[0], vt_ref.at[0], m_ref, l_ref, acc_ref,
                 q_idx=pl.program_id(1), tq=tq, groups=MLA_HEADS, scale=MLA_SCALE)
    for hh in range(MLA_HEADS):
        cols = slice(hh * tq, (hh + 1) * tq)
        o = acc_ref[:, cols] / l_ref[:, cols]
        o_ref[0, :, hh * LANE:(hh + 1) * LANE] = o.T.astype(BF16)


def _flash_mla(qcat, kcat, vt, tq):
    b, s, _ = qcat.shape
    return pl.pallas_call(
        functools.partial(_flash_mla_kernel, tq=tq),
        grid=(b, s // tq),
        in_specs=[pl.BlockSpec((1, tq, MLA_HEADS * QCAT), lambda bi, i: (bi, i, 0)),
                  pl.BlockSpec((1, s, QCAT), lambda bi, i: (bi, 0, 0)),
                  pl.BlockSpec((1, s // tq, LANE, tq), lambda bi, i: (bi, 0, 0, 0))],
        out_specs=pl.BlockSpec((1, tq, MLA_HEADS * LANE), lambda bi, i: (bi, i, 0)),
        out_shape=jax.ShapeDtypeStruct((b, s, MLA_HEADS * LANE), BF16),
        scratch_shapes=_flash_scratch(MLA_HEADS * tq, QCAT),
        compiler_params=_cparams(("arbitrary", "arbitrary"), 48),
        name="flash_mla",
    )(qcat, kcat, vt)


def _diff_lambda(lq1_ref, lk1_ref, lq2_ref, lk2_ref, lam_init):
    a = jnp.sum(lq1_ref[...] * lk1_ref[...], -1, keepdims=True)
    b = jnp.sum(lq2_ref[...] * lk2_ref[...], -1, keepdims=True)
    return jnp.exp(a) - jnp.exp(b) + lam_init


def _diff_mask_rows(tile):
    lane = lax.broadcasted_iota(jnp.int32, tile.shape, 1)
    scaled = tile * jnp.asarray(DIFF_SCALE, tile.dtype)
    zero = jnp.zeros_like(scaled)
    return jnp.where(lane < DIFF_HD, scaled, zero), jnp.where(lane >= DIFF_HD, scaled, zero)


def _flash_diff_kernel(q_ref, k_ref, vt_ref, lq1_ref, lk1_ref, lq2_ref, lk2_ref, ngc_ref,
                       o_ref, qs_ref, m_ref, l_ref, acc_ref, *, tq, lam_init):
    for g in range(DIFF_GROUP):
        qa, qb = _diff_mask_rows(q_ref[0, :, g * LANE:(g + 1) * LANE])
        qs_ref[(2 * g) * tq:(2 * g + 1) * tq, :] = qa
        qs_ref[(2 * g + 1) * tq:(2 * g + 2) * tq, :] = qb
    _flash_sweep(qs_ref, k_ref.at[0], vt_ref.at[0, 0], m_ref, l_ref, acc_ref,
                 q_idx=pl.program_id(2), tq=tq, groups=2 * DIFF_GROUP, scale=None)
    lam = _diff_lambda(lq1_ref, lk1_ref, lq2_ref, lk2_ref, lam_init)
    for g in range(DIFF_GROUP):
        c1 = slice((2 * g) * tq, (2 * g + 1) * tq)
        c2 = slice((2 * g + 1) * tq, (2 * g + 2) * tq)
        o = acc_ref[:, c1] / l_ref[:, c1] - lam * (acc_ref[:, c2] / l_ref[:, c2])
        o = o * lax.rsqrt(jnp.mean(o * o, 0, keepdims=True) + EPS) * ngc_ref[...] * (1.0 - lam_init)
        o_ref[0, :, g * LANE:(g + 1) * LANE] = o.T.astype(BF16)


def _flash_diff(q, k, vt, lam_vecs, norm_col, lam_init, tq):
    b, s, _ = q.shape
    gw = DIFF_GROUP * LANE
    vec = _const_spec((1, DIFF_HD))
    return pl.pallas_call(
        functools.partial(_flash_diff_kernel, tq=tq, lam_init=lam_init),
        grid=(b, DIFF_KVH, s // tq),
        in_specs=[pl.BlockSpec((1, tq, gw), lambda bi, kh, i: (bi, i, kh)),
                  pl.BlockSpec((1, s, LANE), lambda bi, kh, i: (bi, 0, kh)),
                  pl.BlockSpec((1, 1, s // tq, LANE, tq), lambda bi, kh, i: (bi, kh, 0, 0, 0)),
                  vec, vec, vec, vec, _const_spec((LANE, 1))],
        out_specs=pl.BlockSpec((1, tq, gw), lambda bi, kh, i: (bi, i, kh)),
        out_shape=jax.ShapeDtypeStruct((b, s, DIFF_KVH * gw), BF16),
        scratch_shapes=_flash_scratch(2 * DIFF_GROUP * tq, LANE),
        compiler_params=_cparams(("arbitrary", "arbitrary", "arbitrary"), 48),
        name="flash_diff",
    )(q, k, vt, *lam_vecs, norm_col)


def _online_update(s, vals, m_ref, l_ref, acc_ref):
    w = s.shape[1] // len(vals)
    m_old = m_ref[...]
    m_new = jnp.maximum(m_old, jnp.max(s, -1, keepdims=True))
    p = jnp.exp(s - m_new)
    alpha = jnp.exp(m_old - m_new)
    l_ref[...] = alpha * l_ref[...] + jnp.sum(p, -1, keepdims=True)
    pv = None
    for j, vj in enumerate(vals):
        t = _dot(p[:, j * w:(j + 1) * w].astype(BF16), vj)
        pv = t if pv is None else pv + t
    acc_ref[...] = alpha * acc_ref[...] + pv
    m_ref[...] = m_new


def _decode_mla_kernel(pt_ref, q_ref, knew_ref, *rest, pps, t_len):
    ckv_refs, krt_refs = rest[:pps], rest[pps:2 * pps]
    o_ref, m_ref, l_ref, acc_ref = rest[2 * pps:]
    c = pl.program_id(1)

    @pl.when(c == 0)
    def _():
        m_ref[...] = jnp.full(m_ref.shape, NEG_INF, F32)
        l_ref[...] = jnp.zeros(l_ref.shape, F32)
        acc_ref[...] = jnp.zeros(acc_ref.shape, F32)

    q = q_ref[0]
    q_lat, q_rope = q[:, :LANE], q[:, LANE:LANE + MLA_ROPE]
    vals, scores = [], []
    for j in range(pps):
        ckv = ckv_refs[j][...].astype(BF16)
        vals.append(ckv)
        scores.append(_dot_nt(q_lat, ckv) + _dot(q_rope, krt_refs[j][...].astype(BF16)))
    _online_update(jnp.concatenate(scores, -1) * MLA_SCALE, vals, m_ref, l_ref, acc_ref)

    @pl.when(c == pl.num_programs(1) - 1)
    def _():
        kn = knew_ref[0]
        s = _dot_nt(q, kn) * MLA_SCALE
        t = lax.broadcasted_iota(jnp.int32, s.shape, 0) % t_len
        j = lax.broadcasted_iota(jnp.int32, s.shape, 1)
        _online_update(jnp.where(j <= t, s, NEG_INF), [kn[:, :LANE]], m_ref, l_ref, acc_ref)
        o_ref[0] = (acc_ref[...] / l_ref[...]).astype(BF16)


def _decode_mla(page_table, q_rows, k_new, cache_ckv, cache_krt, layer, t_len, pps):
    db, rows, _ = q_rows.shape
    n_pages = page_table.shape[1]
    pt = page_table.reshape(-1)

    def page(shape, j):
        return pl.BlockSpec((None, None) + shape,
                            lambda b, c, pt_ref: (layer, pt_ref[b * n_pages + c * pps + j], 0, 0))

    grid_spec = pltpu.PrefetchScalarGridSpec(
        num_scalar_prefetch=1,
        grid=(db, n_pages // pps),
        in_specs=[pl.BlockSpec((1, rows, QCAT), lambda b, c, pt_ref: (b, 0, 0)),
                  pl.BlockSpec((1, PAGE, QCAT), lambda b, c, pt_ref: (b, 0, 0))]
                 + [page((PAGE, MLA_KV_RANK), j) for j in range(pps)]
                 + [page((MLA_ROPE, PAGE), j) for j in range(pps)],
        out_specs=pl.BlockSpec((1, rows, LANE), lambda b, c, pt_ref: (b, 0, 0)),
        scratch_shapes=[pltpu.VMEM((rows, 1), F32), pltpu.VMEM((rows, 1), F32),
                        pltpu.VMEM((rows, LANE), F32)],
    )
    return pl.pallas_call(
        functools.partial(_decode_mla_kernel, pps=pps, t_len=t_len),
        grid_spec=grid_spec,
        out_shape=jax.ShapeDtypeStruct((db, rows, LANE), BF16),
        compiler_params=_cparams(("arbitrary", "arbitrary"), 32),
        name="decode_mla",
    )(pt, q_rows, k_new, *([cache_ckv] * pps), *([cache_krt] * pps))


def _decode_diff_kernel(pt_ref, q_ref, knew_ref, vnew_ref, lq1_ref, lk1_ref, lq2_ref, lk2_ref, ng_ref,
                        *rest, pps, t_len, lam_init):
    k_refs, v_refs = rest[:pps], rest[pps:2 * pps]
    o_ref, qs_ref, m_ref, l_ref, acc_ref = rest[2 * pps:]
    c = pl.program_id(1)
    half = DIFF_GROUP * t_len
    per_kh = 2 * half

    @pl.when(c == 0)
    def _():
        m_ref[...] = jnp.full(m_ref.shape, NEG_INF, F32)
        l_ref[...] = jnp.zeros(l_ref.shape, F32)
        acc_ref[...] = jnp.zeros(acc_ref.shape, F32)
        for kh in range(DIFF_KVH):
            qa, qb = _diff_mask_rows(q_ref[0, kh * half:(kh + 1) * half, :])
            qs_ref[kh * per_kh:kh * per_kh + half, :] = qa
            qs_ref[kh * per_kh + half:(kh + 1) * per_kh, :] = qb

    def update(k_tiles, v_tiles, new_tokens):
        s = jnp.concatenate([_dot_nt(qs_ref[...], kt) for kt in k_tiles], -1)
        row = lax.broadcasted_iota(jnp.int32, s.shape, 0)
        col = lax.broadcasted_iota(jnp.int32, s.shape, 1)
        keep = (col % DIFF_KVH) == (row // per_kh)
        if new_tokens:
            keep = keep & ((col // DIFF_KVH) <= (row % t_len))
        _online_update(jnp.where(keep, s, NEG_INF), v_tiles, m_ref, l_ref, acc_ref)

    update([r[...].astype(BF16) for r in k_refs], [r[...].astype(BF16) for r in v_refs], False)

    @pl.when(c == pl.num_programs(1) - 1)
    def _():
        update([knew_ref[0]], [vnew_ref[0]], True)
        lam = _diff_lambda(lq1_ref, lk1_ref, lq2_ref, lk2_ref, lam_init)
        for kh in range(DIFF_KVH):
            r1 = slice(kh * per_kh, kh * per_kh + half)
            r2 = slice(kh * per_kh + half, (kh + 1) * per_kh)
            o = acc_ref[r1, :] / l_ref[r1, :] - lam * (acc_ref[r2, :] / l_ref[r2, :])
            o = _rms(o, ng_ref[...]) * (1.0 - lam_init)
            o_ref[0, kh * half:(kh + 1) * half, :] = o.astype(BF16)


def _decode_diff(page_table, q_rows, k_new, v_new, lam_vecs, norm_g, cache_k, cache_v,
                 layer, t_len, lam_init, pps):
    db, rows, _ = q_rows.shape
    n_pages = page_table.shape[1]
    pt = page_table.reshape(-1)
    prow = DIFF_KVH * PAGE

    def page(j):
        return pl.BlockSpec((None, None, prow, LANE),
                            lambda b, c, pt_ref: (layer, pt_ref[b * n_pages + c * pps + j], 0, 0))

    const = lambda shape: pl.BlockSpec(shape, lambda b, c, pt_ref: (0,) * len(shape))
    per_b = lambda shape: pl.BlockSpec(shape, lambda b, c, pt_ref: (b, 0, 0))
    grid_spec = pltpu.PrefetchScalarGridSpec(
        num_scalar_prefetch=1,
        grid=(db, n_pages // pps),
        in_specs=[per_b((1, rows, LANE)), per_b((1, prow, LANE)), per_b((1, prow, LANE)),
                  const((1, DIFF_HD)), const((1, DIFF_HD)), const((1, DIFF_HD)), const((1, DIFF_HD)),
                  const((1, LANE))]
                 + [page(j) for j in range(pps)] + [page(j) for j in range(pps)],
        out_specs=per_b((1, rows, LANE)),
        scratch_shapes=[pltpu.VMEM((2 * rows, LANE), BF16), pltpu.VMEM((2 * rows, 1), F32),
                        pltpu.VMEM((2 * rows, 1), F32), pltpu.VMEM((2 * rows, LANE), F32)],
    )
    return pl.pallas_call(
        functools.partial(_decode_diff_kernel, pps=pps, t_len=t_len, lam_init=lam_init),
        grid_spec=grid_spec,
        out_shape=jax.ShapeDtypeStruct((db, rows, LANE), BF16),
        compiler_params=_cparams(("arbitrary", "arbitrary"), 48),
        name="decode_diff",
    )(pt, q_rows, k_new, v_new, *lam_vecs, norm_g, *([cache_k] * pps), *([cache_v] * pps))


def _odd_in_kernel(x_ref, w_ref, q_ref, k_ref, v_ref, kb_ref, vb_ref):
    h = _dot(x_ref[...].astype(BF16), w_ref[...])
    nq = DIFF_HEADS * 2 * DIFF_HD
    nkv = DIFF_KVH * 2 * DIFF_HD
    q_ref[...] = h[:, :nq].astype(BF16)
    k = h[:, nq:nq + nkv]
    v = h[:, nq + nkv:]
    k_ref[...] = k
    v_ref[...] = v
    kb_ref[...] = k.astype(BF16)
    vb_ref[...] = v.astype(BF16)


def _odd_in(x2d, w_in, tm):
    m = x2d.shape[0]
    nq = DIFF_HEADS * 2 * DIFF_HD
    nkv = DIFF_KVH * 2 * DIFF_HD
    row = lambda w: pl.BlockSpec((tm, w), lambda i: (i, 0))
    return pl.pallas_call(
        _odd_in_kernel,
        grid=(m // tm,),
        in_specs=[row(D_MODEL), _const_spec((D_MODEL, nq + 2 * nkv))],
        out_specs=[row(nq), row(nkv), row(nkv), row(nkv), row(nkv)],
        out_shape=[jax.ShapeDtypeStruct((m, nq), BF16),
                   jax.ShapeDtypeStruct((m, nkv), F32), jax.ShapeDtypeStruct((m, nkv), F32),
                   jax.ShapeDtypeStruct((m, nkv), BF16), jax.ShapeDtypeStruct((m, nkv), BF16)],
        compiler_params=_cparams(("arbitrary",), 40),
        name="odd_in",
    )(x2d, w_in)


def _post_kernel(*refs, even):
    if even:
        (x_ref, a_ref, pool_ref, wuv_ref, wmix_ref, g1_ref, b1_ref, g2_ref, b2_ref,
         w1_ref, fb1_ref, w2_ref, fb2_ref, o_ref) = refs
        o = _dot(a_ref[...], wuv_ref[...]).astype(BF16)
        mix = _dot(pool_ref[...], wmix_ref[:POOL_DIM, :]) + _dot(o, wmix_ref[POOL_DIM:, :])
    else:
        (x_ref, a_ref, wmix_ref, g1_ref, b1_ref, g2_ref, b2_ref,
         w1_ref, fb1_ref, w2_ref, fb2_ref, o_ref) = refs
        mix = _dot(a_ref[...], wmix_ref[...])
    x1 = _ln(ALPHA * x_ref[...] + mix, g1_ref[...], b1_ref[...])
    x1b = x1.astype(BF16)
    ffn = None
    for c in range(D_FF // FF_CHUNK):
        cols = slice(c * FF_CHUNK, (c + 1) * FF_CHUNK)
        hdn = jnp.square(jnp.maximum(_dot(x1b, w1_ref[:, cols]) + fb1_ref[:, cols], 0.0))
        part = _dot(hdn.astype(BF16), w2_ref[cols, :])
        ffn = part if ffn is None else ffn + part
    o_ref[...] = _ln(ALPHA * x1 + (ffn + fb2_ref[...]), g2_ref[...], b2_ref[...])


def _post(x2d, a, extra, wts, even, tm):
    m = x2d.shape[0]
    row = lambda w: pl.BlockSpec((tm, w), lambda i: (i, 0))
    vec = _const_spec((1, D_MODEL))
    ins = [x2d, a]
    specs = [row(D_MODEL), row(a.shape[1])]
    if even:
        ins += [extra, wts["w_uv"]]
        specs += [row(POOL_DIM), _const_spec((MLA_HEADS * LANE, MLA_HEADS * MLA_V))]
    ins += [wts["w_mix"], wts["g1"], wts["b1"], wts["g2"], wts["b2"],
            wts["w1"], wts["fb1"], wts["w2"], wts["fb2"]]
    specs += [_const_spec((D_MODEL, D_MODEL)), vec, vec, vec, vec,
              _const_spec((D_MODEL, D_FF)), _const_spec((1, D_FF)), _const_spec((D_FF, D_MODEL)), vec]
    return pl.pallas_call(
        functools.partial(_post_kernel, even=even),
        grid=(m // tm,),
        in_specs=specs,
        out_specs=row(D_MODEL),
        out_shape=jax.ShapeDtypeStruct((m, D_MODEL), F32),
        compiler_params=_cparams(("arbitrary",), 56),
        name="post_even" if even else "post_odd",
    )(*ins)


def _rope_tables(pos):
    half = MLA_ROPE // 2
    freq = ROPE_THETA ** (-jnp.arange(half, dtype=F32) / half)
    ang = pos[:, None] * freq[None, :]
    cos, sin = jnp.cos(ang), jnp.sin(ang)
    cc = jnp.concatenate([cos, cos], -1)
    ss = jnp.concatenate([-sin, sin], -1)
    pad = jnp.zeros((pos.shape[0], LANE - MLA_ROPE), F32)
    return (jnp.tile(cc, (1, MLA_HEADS)), jnp.tile(ss, (1, MLA_HEADS)),
            jnp.concatenate([cc, pad], -1), jnp.concatenate([ss, pad], -1))


def _block_diag(blocks):
    h, r, c = blocks.shape
    eye = jnp.eye(h, dtype=blocks.dtype)
    return (blocks[:, :, None, :] * eye[:, None, :, None]).reshape(h * r, h * c)


def _even_weights(w_in, q_norm, w_uq, kv_norm, w_uk, w_uv, w_out):
    half = MLA_ROPE // 2
    kr0 = POOL_DIM + MLA_Q_RANK + MLA_KV_RANK
    zpad = jnp.zeros((D_MODEL, LANE - MLA_ROPE), F32)
    w_in_p = jnp.concatenate(
        [w_in, zpad, w_in[:, kr0 + half:], w_in[:, kr0:kr0 + half], zpad], -1).astype(BF16)
    uq = w_uq.reshape(MLA_Q_RANK, MLA_HEADS, MLA_NOPE + MLA_ROPE)
    nope = uq[:, :, :MLA_NOPE].reshape(MLA_Q_RANK, MLA_HEADS * MLA_NOPE)
    rope = uq[:, :, MLA_NOPE:]
    rope_a = rope.reshape(MLA_Q_RANK, MLA_HEADS * MLA_ROPE)
    rope_b = jnp.concatenate([rope[:, :, half:], rope[:, :, :half]], -1).reshape(MLA_Q_RANK, MLA_HEADS * MLA_ROPE)
    place = jnp.concatenate([jnp.eye(MLA_ROPE, dtype=F32), jnp.zeros((MLA_ROPE, LANE - MLA_ROPE), F32)], -1)
    return {
        "w_in": w_in_p,
        "q_norm": q_norm.reshape(1, -1),
        "w_uq": jnp.concatenate([nope, rope_a, rope_b], -1).astype(BF16),
        "kv_norm": kv_norm.reshape(1, -1),
        "w_bd": _block_diag(jnp.transpose(w_uk, (1, 2, 0))).astype(BF16),
        "place": _block_diag(jnp.broadcast_to(place, (MLA_HEADS,) + place.shape)).astype(BF16),
        "w_uv": _block_diag(jnp.transpose(w_uv, (1, 0, 2))).astype(BF16),
        "w_mix": w_out.astype(BF16),
    }


def _ffn_weights(g1, b1, g2, b2, w1, fb1, w2, fb2):
    r = lambda v: v.reshape(1, -1)
    return {"g1": r(g1), "b1": r(b1), "g2": r(g2), "b2": r(b2),
            "w1": w1.astype(BF16), "fb1": r(fb1), "w2": w2.astype(BF16), "fb2": r(fb2)}


def _token_tile(m):
    return TOKEN_TILE if m % TOKEN_TILE == 0 else m


def kernel(x_prompt, x_sample, cache_mla_ckv, cache_mla_krope, cache_diff_k, cache_diff_v, state_pool,
           page_table, w_in_even, pool_w, pool_scale, mla_q_norm, mla_w_uq, mla_kv_norm, mla_w_uk,
           mla_w_uv, w_out_even, w_in_odd, diff_lq1, diff_lk1, diff_lq2, diff_lk2, diff_norm, w_out_odd,
           ln1_g, ln1_b, ln2_g, ln2_b, mlp_w1, mlp_b1, mlp_w2, mlp_b2):
    bsz, s_len, d = x_prompt.shape
    db, t_len, _ = x_sample.shape
    n_pages = page_table.shape[1]
    past = n_pages * PAGE
    mp, ms = bsz * s_len, db * t_len
    tmp, tms = _token_tile(mp), _token_tile(ms)
    tq = min(FLASH_TQ, s_len)
    pps = min(PAGES_PER_STEP, n_pages)
    n_pool = cache_diff_k.shape[1]
    kvw = DIFF_KVH * 2 * DIFF_HD
    cache_dk = cache_diff_k.reshape(cache_diff_k.shape[0], n_pool, DIFF_KVH * PAGE, LANE)
    cache_dv = cache_diff_v.reshape(cache_diff_v.shape[0], n_pool, DIFF_KVH * PAGE, LANE)
    cache_krt = jnp.swapaxes(cache_mla_krope, 2, 3)

    tabs_p = _rope_tables(jnp.arange(s_len, dtype=F32))
    tabs_s = tuple(jnp.tile(t, (tms // t_len, 1)) for t in _rope_tables(past + jnp.arange(t_len, dtype=F32)))
    pad_rows = lambda a, n: jnp.pad(a, ((0, 0), (0, n - a.shape[1]), (0, 0)))
    hist = 24 - t_len

    xp = x_prompt.reshape(mp, d)
    xs = x_sample.reshape(ms, d)
    outs = {k: [] for k in ("p_ckv", "p_kr", "p_dk", "p_dv", "p_pool", "s_ckv", "s_kr", "s_dk", "s_dv", "s_pool")}
    for i in range(DEPTH):
        ffn = _ffn_weights(ln1_g[i], ln1_b[i], ln2_g[i], ln2_b[i], mlp_w1[i], mlp_b1[i], mlp_w2[i], mlp_b2[i])
        if i % 2 == 0:
            e = i // 2
            wts = _even_weights(w_in_even[e], mla_q_norm[e], mla_w_uq[e], mla_kv_norm[e], mla_w_uk[e],
                                mla_w_uv[e], w_out_even[e])
            wts.update(ffn)
            pw = pool_w[e].astype(BF16)
            psc = pool_scale[e].reshape(1, -1)
            u, qcat, kcat, ckv, kr = _even_in(xp, wts, tabs_p, s_len // tmp if s_len % tmp == 0 else 1, tmp)
            u3 = u.reshape(bsz, s_len, POOL_DIM)
            pool_o = _pool(u3, pw, psc, 0, 1).reshape(mp, POOL_DIM)
            kc3 = kcat.reshape(bsz, s_len, QCAT)
            vt = jnp.transpose(kc3[:, :, :LANE].reshape(bsz, s_len // tq, tq, LANE), (0, 1, 3, 2))
            o_lat = _flash_mla(qcat.reshape(bsz, s_len, -1), kc3, vt, tq)
            xp = _post(xp, o_lat.reshape(mp, -1), pool_o, wts, True, tmp)
            outs["p_ckv"].append(ckv.reshape(bsz, s_len, -1))
            outs["p_kr"].append(kr.reshape(bsz, s_len, -1))
            outs["p_pool"].append(u3[:, s_len - POOL_STATE:])
            u, qcat, kcat, ckv, kr = _even_in(xs, wts, tabs_s, 1, tms)
            u3 = u.reshape(db, t_len, POOL_DIM)
            u_ext = jnp.concatenate([state_pool[e], u3], axis=1)
            u_pad = jnp.pad(u_ext, ((0, 0), (24 - u_ext.shape[1], 0), (0, 0)))
            bb = db
            while bb * u_pad.shape[1] > 2048 and bb % 2 == 0:
                bb //= 2
            pool_o = _pool(u_pad, pw, psc, past - hist, bb)[:, hist:].reshape(ms, POOL_DIM)
            q_rows = jnp.transpose(qcat.reshape(db, t_len, MLA_HEADS, QCAT), (0, 2, 1, 3))
            o_rows = _decode_mla(page_table, q_rows.reshape(db, MLA_HEADS * t_len, QCAT),
                                 pad_rows(kcat.reshape(db, t_len, QCAT), PAGE),
                                 cache_mla_ckv, cache_krt, e, t_len, pps)
            o_lat = jnp.transpose(o_rows.reshape(db, MLA_HEADS, t_len, LANE), (0, 2, 1, 3)).reshape(ms, -1)
            xs = _post(xs, o_lat, pool_o, wts, True, tms)
            outs["s_ckv"].append(ckv.reshape(db, t_len, -1))
            outs["s_kr"].append(kr.reshape(db, t_len, -1))
            outs["s_pool"].append(u_ext[:, u_ext.shape[1] - POOL_STATE:])
        else:
            o = i // 2
            lam_init = 0.8 - 0.6 * math.exp(-0.3 * i)
            lam_vecs = tuple(v[o].reshape(1, -1) for v in (diff_lq1, diff_lk1, diff_lq2, diff_lk2))
            norm_g = diff_norm[o].reshape(1, -1)
            wts = dict(ffn)
            wts["w_mix"] = w_out_odd[o].astype(BF16)
            w_in = w_in_odd[o].astype(BF16)
            q, k, v, kb, vb = _odd_in(xp, w_in, tmp)
            vt = jnp.transpose(vb.reshape(bsz, s_len // tq, tq, DIFF_KVH, LANE), (0, 3, 1, 4, 2))
            att = _flash_diff(q.reshape(bsz, s_len, -1), kb.reshape(bsz, s_len, kvw), vt,
                              lam_vecs, norm_g.reshape(-1, 1), lam_init, tq)
            xp = _post(xp, att.reshape(mp, -1), None, wts, False, tmp)
            outs["p_dk"].append(k.reshape(bsz, s_len, DIFF_KVH, 2 * DIFF_HD))
            outs["p_dv"].append(v.reshape(bsz, s_len, DIFF_KVH, 2 * DIFF_HD))
            q, k, v, kb, vb = _odd_in(xs, w_in, tms)
            heads = DIFF_KVH * DIFF_GROUP
            q_rows = jnp.transpose(q.reshape(db, t_len, heads, LANE), (0, 2, 1, 3)).reshape(db, heads * t_len, LANE)
            new_rows = lambda a: pad_rows(a.reshape(db, t_len * DIFF_KVH, LANE), DIFF_KVH * PAGE)
            o_rows = _decode_diff(page_table, q_rows, new_rows(kb), new_rows(vb), lam_vecs, norm_g,
                                  cache_dk, cache_dv, o, t_len, lam_init, pps)
            att = jnp.transpose(o_rows.reshape(db, heads, t_len, LANE), (0, 2, 1, 3)).reshape(ms, -1)
            xs = _post(xs, att, None, wts, False, tms)
            outs["s_dk"].append(k.reshape(db, t_len, DIFF_KVH, 2 * DIFF_HD))
            outs["s_dv"].append(v.reshape(db, t_len, DIFF_KVH, 2 * DIFF_HD))

    st = lambda name: jnp.stack(outs[name])
    return (xp.reshape(bsz, s_len, d), xs.reshape(db, t_len, d),
            st("p_ckv"), st("p_kr"), st("p_dk"), st("p_dv"), st("p_pool"),
            st("s_ckv"), st("s_kr"), st("s_dk"), st("s_dv"), st("s_pool"))
```

```python
import functools
import math

import jax
import jax.numpy as jnp
from jax import lax
from jax.experimental import pallas as pl
from jax.experimental.pallas import tpu as pltpu

F32 = jnp.float32
BF16 = jnp.bfloat16

D_MODEL = 1024
DEPTH = 4
PAGE = 128
POOL_DIM = 512
POOL_WINDOWS = (2, 4, 8, 16)
POOL_GD = 128
POOL_PAD = 16
POOL_STATE = 15
MLA_HEADS = 8
MLA_NOPE = 64
MLA_ROPE = 32
MLA_V = 64
MLA_Q_RANK = 256
MLA_KV_RANK = 128
MLA_SCALE = (MLA_NOPE + MLA_ROPE) ** -0.5
ROPE_THETA = 10000.0
DIFF_HEADS = 8
DIFF_HD = 64
DIFF_KVH = 2
DIFF_GROUP = 4
DIFF_SCALE = DIFF_HD ** -0.5
D_FF = 4096
ALPHA = (2 * DEPTH) ** 0.25
EPS = 1e-5
QCAT = 256
EVEN_IN_P = 1152

LANE = 128
TOKEN_TILE = 512
FLASH_TQ = 256
FF_CHUNK = 1024
MLA_PAGES_PER_STEP = 64
DIFF_PAGES_PER_STEP = 32
NEG_INF = float("-inf")


def _cparams(sem, vmem_mib):
    return pltpu.CompilerParams(dimension_semantics=sem, vmem_limit_bytes=vmem_mib * 1024 * 1024)


def _const_spec(shape):
    nd = len(shape)
    return pl.BlockSpec(shape, lambda *_: (0,) * nd, pipeline_mode=pl.Buffered(1))


def _dot(a, b):
    return jnp.dot(a, b, preferred_element_type=F32)


def _dot_nt(a, b):
    return lax.dot_general(a, b, (((1,), (1,)), ((), ())), preferred_element_type=F32)


def _rms(x, g):
    return x * lax.rsqrt(jnp.mean(x * x, -1, keepdims=True) + EPS) * g


def _ln(x, g, b):
    mu = jnp.mean(x, -1, keepdims=True)
    xc = x - mu
    var = jnp.mean(xc * xc, -1, keepdims=True)
    return xc * lax.rsqrt(var + EPS) * g + b


def _even_in_kernel(x_ref, w_in_ref, qn_ref, wuq_ref, kvn_ref, wbd_ref, place_ref,
                    cq_ref, sq_ref, ck_ref, sk_ref,
                    u_ref, qcat_ref, kcat_ref, ckv_ref, kr_ref):
    h = _dot(x_ref[...].astype(BF16), w_in_ref[...])
    u_ref[...] = h[:, :POOL_DIM]
    cqn = _rms(h[:, 512:768], qn_ref[...])
    q = _dot(cqn.astype(BF16), wuq_ref[...])
    rot = q[:, 512:768] * cq_ref[...] + q[:, 768:1024] * sq_ref[...]
    qlat = _dot(q[:, :512].astype(BF16), wbd_ref[...])
    qrope = _dot(rot.astype(BF16), place_ref[...])
    for hh in range(MLA_HEADS):
        qcat_ref[:, hh * QCAT:hh * QCAT + LANE] = qlat[:, hh * LANE:(hh + 1) * LANE].astype(BF16)
        qcat_ref[:, hh * QCAT + LANE:(hh + 1) * QCAT] = qrope[:, hh * LANE:(hh + 1) * LANE].astype(BF16)
    ckvn = _rms(h[:, 768:896], kvn_ref[...])
    ckv_ref[...] = ckvn
    krr = h[:, 896:1024] * ck_ref[...] + h[:, 1024:1152] * sk_ref[...]
    kr_ref[...] = krr[:, :MLA_ROPE]
    kcat_ref[:, :LANE] = ckvn.astype(BF16)
    kcat_ref[:, LANE:] = krr.astype(BF16)


def _even_in(x2d, wts, tabs, n_tab_blocks, tm):
    m = x2d.shape[0]
    row = lambda w: pl.BlockSpec((tm, w), lambda i: (i, 0))
    tab = lambda w: pl.BlockSpec((tm, w), lambda i: (i % n_tab_blocks, 0))
    return pl.pallas_call(
        _even_in_kernel,
        grid=(m // tm,),
        in_specs=[row(D_MODEL), _const_spec((D_MODEL, EVEN_IN_P)), _const_spec((1, MLA_Q_RANK)),
                  _const_spec((MLA_Q_RANK, 1024)), _const_spec((1, MLA_KV_RANK)),
                  _const_spec((512, 1024)), _const_spec((256, 1024)),
                  tab(256), tab(256), tab(LANE), tab(LANE)],
        out_specs=[row(POOL_DIM), row(MLA_HEADS * QCAT), row(QCAT), row(MLA_KV_RANK), row(MLA_ROPE)],
        out_shape=[jax.ShapeDtypeStruct((m, POOL_DIM), F32),
                   jax.ShapeDtypeStruct((m, MLA_HEADS * QCAT), BF16),
                   jax.ShapeDtypeStruct((m, QCAT), BF16),
                   jax.ShapeDtypeStruct((m, MLA_KV_RANK), F32),
                   jax.ShapeDtypeStruct((m, MLA_ROPE), F32)],
        compiler_params=_cparams(("arbitrary",), 40),
        name="even_in",
    )(x2d, wts["w_in"], wts["q_norm"], wts["w_uq"], wts["kv_norm"], wts["w_bd"], wts["place"], *tabs)


def _pool_kernel(u_ref, w_ref, sc_ref, o_ref, ext_ref, *, start_pos):
    bb, seq, _ = u_ref.shape
    ext_ref[:, :POOL_PAD, :] = jnp.zeros((bb, POOL_PAD, POOL_DIM), F32)
    ext_ref[:, POOL_PAD:, :] = u_ref[...]
    pos = start_pos + lax.broadcasted_iota(jnp.int32, (1, seq, 1), 1).astype(F32)
    for g, win in enumerate(POOL_WINDOWS):
        lanes = slice(g * POOL_GD, (g + 1) * POOL_GD)
        tot = ext_ref[:, POOL_PAD:POOL_PAD + seq, lanes]
        for j in range(1, win):
            tot = tot + ext_ref[:, POOL_PAD - j:POOL_PAD - j + seq, lanes]
        cnt = jnp.minimum(pos + 1.0, float(win))
        d = tot / cnt - u_ref[:, :, lanes]
        y = _dot(d.reshape(bb * seq, POOL_GD).astype(BF16), w_ref[g])
        o_ref[:, :, lanes] = (y.reshape(bb, seq, POOL_GD) * sc_ref[:, lanes]).astype(BF16)


def _pool(u_ext, pool_w, pool_scale, start_pos, bb):
    nb, seq, _ = u_ext.shape
    blk = pl.BlockSpec((bb, seq, POOL_DIM), lambda i: (i, 0, 0))
    return pl.pallas_call(
        functools.partial(_pool_kernel, start_pos=float(start_pos)),
        grid=(nb // bb,),
        in_specs=[blk, _const_spec((len(POOL_WINDOWS), POOL_GD, POOL_GD)), _const_spec((1, POOL_DIM))],
        out_specs=blk,
        out_shape=jax.ShapeDtypeStruct((nb, seq, POOL_DIM), BF16),
        scratch_shapes=[pltpu.VMEM((bb, POOL_PAD + seq, POOL_DIM), F32)],
        compiler_params=_cparams(("arbitrary",), 40),
        name="pool_mix",
    )(u_ext, pool_w, pool_scale)


def _flash_sweep(qs_ref, k_ref, vt_ref, m_ref, l_ref, acc_ref, st_ref, *, q_idx, tq, groups, scale):
    rows = groups * tq
    m_ref[...] = jnp.full((1, rows), NEG_INF, F32)
    l_ref[...] = jnp.zeros((1, rows), F32)
    acc_ref[...] = jnp.zeros((LANE, rows), F32)
    q = qs_ref[...]

    def scores(j, diagonal):
        start = pl.multiple_of(j * tq, tq)
        st = _dot_nt(k_ref[pl.ds(start, tq), :], q)
        if scale is not None:
            st = st * scale
        if diagonal:
            key = lax.broadcasted_iota(jnp.int32, (tq, rows), 0)
            t = lax.broadcasted_iota(jnp.int32, (tq, rows), 1) % tq
            st = jnp.where(key <= t, st, NEG_INF)
        return st

    def consume(j, st):
        m_old = m_ref[...]
        m_new = jnp.maximum(m_old, jnp.max(st, 0, keepdims=True))
        p = jnp.exp(st - m_new)
        alpha = jnp.exp(m_old - m_new)
        l_ref[...] = alpha * l_ref[...] + jnp.sum(p, 0, keepdims=True)
        acc_ref[...] = alpha * acc_ref[...] + _dot(vt_ref[j], p.astype(BF16))
        m_ref[...] = m_new

    @pl.when(q_idx == 0)
    def _():
        consume(0, scores(0, True))

    @pl.when(q_idx > 0)
    def _():
        st_ref[...] = scores(0, False)

        def body(j, carry):
            nxt = scores(j + 1, False)
            consume(j, st_ref[...])
            st_ref[...] = nxt
            return carry

        lax.fori_loop(0, q_idx - 1, body, 0)
        nxt = scores(q_idx, True)
        consume(q_idx - 1, st_ref[...])
        consume(q_idx, nxt)


def _flash_scratch(rows, dk, tq):
    return [pltpu.VMEM((rows, dk), BF16), pltpu.VMEM((1, rows), F32), pltpu.VMEM((1, rows), F32),
            pltpu.VMEM((LANE, rows), F32), pltpu.VMEM((tq, rows), F32)]


def _flash_mla_kernel(q_ref, k_ref, vt_ref, o_ref, qs_ref, m_ref, l_ref, acc_ref, st_ref, *, tq):
    for hh in range(MLA_HEADS):
        qs_ref[hh * tq:(hh + 1) * tq, :] = q_ref[0, :, hh * QCAT:(hh + 1) * QCAT]
    _flash_sweep(qs_ref, k_ref.at[0], vt_ref.at[0], m_ref, l_ref, acc_ref, st_ref,
                 q_idx=pl.program_id(1), tq=tq, groups=MLA_HEADS, scale=MLA_SCALE)
    for hh in range(MLA_HEADS):
        cols = slice(hh * tq, (hh + 1) * tq)
        o = acc_ref[:, cols] / l_ref[:, cols]
        o_ref[0, :, hh * LANE:(hh + 1) * LANE] = o.T.astype(BF16)


def _flash_mla(qcat, kcat, vt, tq):
    b, s, _ = qcat.shape
    return pl.pallas_call(
        functools.partial(_flash_mla_kernel, tq=tq),
        grid=(b, s // tq),
        in_specs=[pl.BlockSpec((1, tq, MLA_HEADS * QCAT), lambda bi, i: (bi, i, 0)),
                  pl.BlockSpec((1, s, QCAT), lambda bi, i: (bi, 0, 0)),
                  pl.BlockSpec((1, s // tq, LANE, tq), lambda bi, i: (bi, 0, 0, 0))],
        out_specs=pl.BlockSpec((1, tq, MLA_HEADS * LANE), lambda bi, i: (bi, i, 0)),
        out_shape=jax.ShapeDtypeStruct((b, s, MLA_HEADS * LANE), BF16),
        scratch_shapes=_flash_scratch(MLA_HEADS * tq, QCAT, tq),
        compiler_params=_cparams(("arbitrary", "arbitrary"), 48),
        name="flash_mla",
    )(qcat, kcat, vt)


def _diff_lambda(lq1_ref, lk1_ref, lq2_ref, lk2_ref, lam_init):
    a = jnp.sum(lq1_ref[...] * lk1_ref[...], -1, keepdims=True)
    b = jnp.sum(lq2_ref[...] * lk2_ref[...], -1, keepdims=True)
    return jnp.exp(a) - jnp.exp(b) + lam_init


def _diff_mask_rows(tile):
    lane = lax.broadcasted_iota(jnp.int32, tile.shape, 1)
    scaled = tile * jnp.asarray(DIFF_SCALE, tile.dtype)
    zero = jnp.zeros_like(scaled)
    return jnp.where(lane < DIFF_HD, scaled, zero), jnp.where(lane >= DIFF_HD, scaled, zero)


def _flash_diff_kernel(q_ref, k_ref, vt_ref, lq1_ref, lk1_ref, lq2_ref, lk2_ref, ngc_ref,
                       o_ref, qs_ref, m_ref, l_ref, acc_ref, st_ref, *, tq, lam_init):
    for g in range(DIFF_GROUP):
        qa, qb = _diff_mask_rows(q_ref[0, :, g * LANE:(g + 1) * LANE])
        qs_ref[(2 * g) * tq:(2 * g + 1) * tq, :] = qa
        qs_ref[(2 * g + 1) * tq:(2 * g + 2) * tq, :] = qb
    _flash_sweep(qs_ref, k_ref.at[0], vt_ref.at[0, 0], m_ref, l_ref, acc_ref, st_ref,
                 q_idx=pl.program_id(2), tq=tq, groups=2 * DIFF_GROUP, scale=None)
    lam = _diff_lambda(lq1_ref, lk1_ref, lq2_ref, lk2_ref, lam_init)
    for g in range(DIFF_GROUP):
        c1 = slice((2 * g) * tq, (2 * g + 1) * tq)
        c2 = slice((2 * g + 1) * tq, (2 * g + 2) * tq)
        o = acc_ref[:, c1] / l_ref[:, c1] - lam * (acc_ref[:, c2] / l_ref[:, c2])
        o = o * lax.rsqrt(jnp.mean(o * o, 0, keepdims=True) + EPS) * ngc_ref[...] * (1.0 - lam_init)
        o_ref[0, :, g * LANE:(g + 1) * LANE] = o.T.astype(BF16)


def _flash_diff(q, k, vt, lam_vecs, norm_col, lam_init, tq):
    b, s, _ = q.shape
    gw = DIFF_GROUP * LANE
    vec = _const_spec((1, DIFF_HD))
    return pl.pallas_call(
        functools.partial(_flash_diff_kernel, tq=tq, lam_init=lam_init),
        grid=(b, DIFF_KVH, s // tq),
        in_specs=[pl.BlockSpec((1, tq, gw), lambda bi, kh, i: (bi, i, kh)),
                  pl.BlockSpec((1, s, LANE), lambda bi, kh, i: (bi, 0, kh)),
                  pl.BlockSpec((1, 1, s // tq, LANE, tq), lambda bi, kh, i: (bi, kh, 0, 0, 0)),
                  vec, vec, vec, vec, _const_spec((LANE, 1))],
        out_specs=pl.BlockSpec((1, tq, gw), lambda bi, kh, i: (bi, i, kh)),
        out_shape=jax.ShapeDtypeStruct((b, s, DIFF_KVH * gw), BF16),
        scratch_shapes=_flash_scratch(2 * DIFF_GROUP * tq, LANE, tq),
        compiler_params=_cparams(("arbitrary", "arbitrary", "arbitrary"), 48),
        name="flash_diff",
    )(q, k, vt, *lam_vecs, norm_col)


def _online_update(s, vals, m_ref, l_ref, acc_ref):
    w = s.shape[1] // len(vals)
    m_old = m_ref[...]
    m_new = jnp.maximum(m_old, jnp.max(s, -1, keepdims=True))
    p = jnp.exp(s - m_new)
    alpha = jnp.exp(m_old - m_new)
    l_ref[...] = alpha * l_ref[...] + jnp.sum(p, -1, keepdims=True)
    pv = None
    for j, vj in enumerate(vals):
        t = _dot(p[:, j * w:(j + 1) * w].astype(BF16), vj)
        pv = t if pv is None else pv + t
    acc_ref[...] = alpha * acc_ref[...] + pv
    m_ref[...] = m_new


def _paged_copies(pt_ref, srcs, dsts, sem, b, c, slot, *, pps, n_pages, layer, start):
    base = b * n_pages + c * pps
    for j in range(pps):
        pg = pt_ref[base + j]
        for a, (src, dst) in enumerate(zip(srcs, dsts)):
            cp = pltpu.make_async_copy(src.at[layer, pg], dst(slot, j), sem.at[a, slot])
            if start:
                cp.start()
            else:
                cp.wait()


def _paged_pipeline(fetch):
    b, c = pl.program_id(0), pl.program_id(1)
    nb, nc = pl.num_programs(0), pl.num_programs(1)
    step = b * nc + c
    slot = step % 2

    @pl.when(step == 0)
    def _():
        fetch(b, c, slot, start=True)

    @pl.when(step + 1 < nb * nc)
    def _():
        wrap = c + 1 == nc
        fetch(jnp.where(wrap, b + 1, b), jnp.where(wrap, 0, c + 1), 1 - slot, start=True)

    fetch(b, c, slot, start=False)
    return slot


def _decode_mla_kernel(pt_ref, q_ref, knew_ref, ckv_hbm, krt_hbm, o_ref,
                       kbuf, rbuf, sem, m_ref, l_ref, acc_ref, *, pps, n_pages, layer, t_len):
    c = pl.program_id(1)
    dsts = [lambda s, j: kbuf.at[s, pl.ds(j * PAGE, PAGE), :],
            lambda s, j: rbuf.at[s, :, pl.ds(j * PAGE, PAGE)]]
    slot = _paged_pipeline(functools.partial(_paged_copies, pt_ref, [ckv_hbm, krt_hbm], dsts, sem,
                                             pps=pps, n_pages=n_pages, layer=layer))

    @pl.when(c == 0)
    def _():
        m_ref[...] = jnp.full(m_ref.shape, NEG_INF, F32)
        l_ref[...] = jnp.zeros(l_ref.shape, F32)
        acc_ref[...] = jnp.zeros(acc_ref.shape, F32)

    q = q_ref[0]
    kb = kbuf[slot].astype(BF16)
    s = _dot_nt(q[:, :LANE], kb) + _dot(q[:, LANE:LANE + MLA_ROPE], rbuf[slot].astype(BF16))
    _online_update(s * MLA_SCALE, [kb], m_ref, l_ref, acc_ref)

    @pl.when(c == pl.num_programs(1) - 1)
    def _():
        kn = knew_ref[0]
        sn = _dot_nt(q, kn) * MLA_SCALE
        t = lax.broadcasted_iota(jnp.int32, sn.shape, 0) % t_len
        j = lax.broadcasted_iota(jnp.int32, sn.shape, 1)
        _online_update(jnp.where(j <= t, sn, NEG_INF), [kn[:, :LANE]], m_ref, l_ref, acc_ref)
        o_ref[0] = (acc_ref[...] / l_ref[...]).astype(BF16)


def _decode_mla(page_table, q_rows, k_new, cache_ckv, cache_krt, layer, t_len, pps):
    db, rows, _ = q_rows.shape
    n_pages = page_table.shape[1]
    per_b = lambda shape: pl.BlockSpec(shape, lambda b, c, pt_ref: (b, 0, 0))
    hbm = pl.BlockSpec(memory_space=pl.ANY)
    grid_spec = pltpu.PrefetchScalarGridSpec(
        num_scalar_prefetch=1,
        grid=(db, n_pages // pps),
        in_specs=[per_b((1, rows, QCAT)), per_b((1, PAGE, QCAT)), hbm, hbm],
        out_specs=per_b((1, rows, LANE)),
        scratch_shapes=[pltpu.VMEM((2, pps * PAGE, MLA_KV_RANK), F32),
                        pltpu.VMEM((2, MLA_ROPE, pps * PAGE), F32),
                        pltpu.SemaphoreType.DMA((2, 2)),
                        pltpu.VMEM((rows, 1), F32), pltpu.VMEM((rows, 1), F32),
                        pltpu.VMEM((rows, LANE), F32)],
    )
    return pl.pallas_call(
        functools.partial(_decode_mla_kernel, pps=pps, n_pages=n_pages, layer=layer, t_len=t_len),
        grid_spec=grid_spec,
        out_shape=jax.ShapeDtypeStruct((db, rows, LANE), BF16),
        compiler_params=_cparams(("arbitrary", "arbitrary"), 40),
        name="decode_mla",
    )(page_table.reshape(-1), q_rows, k_new, cache_ckv, cache_krt)


def _decode_diff_kernel(pt_ref, q_ref, knew_ref, vnew_ref, lq1_ref, lk1_ref, lq2_ref, lk2_ref, ng_ref,
                        k_hbm, v_hbm, o_ref, kbuf, vbuf, sem, qs_ref, m_ref, l_ref, acc_ref,
                        *, pps, n_pages, layer, t_len, lam_init):
    c = pl.program_id(1)
    half = DIFF_GROUP * t_len
    per_kh = 2 * half
    prow = DIFF_KVH * PAGE
    dsts = [lambda s, j: kbuf.at[s, pl.ds(j * prow, prow), :],
            lambda s, j: vbuf.at[s, pl.ds(j * prow, prow), :]]
    slot = _paged_pipeline(functools.partial(_paged_copies, pt_ref, [k_hbm, v_hbm], dsts, sem,
                                             pps=pps, n_pages=n_pages, layer=layer))

    @pl.when(c == 0)
    def _():
        m_ref[...] = jnp.full(m_ref.shape, NEG_INF, F32)
        l_ref[...] = jnp.zeros(l_ref.shape, F32)
        acc_ref[...] = jnp.zeros(acc_ref.shape, F32)
        for kh in range(DIFF_KVH):
            qa, qb = _diff_mask_rows(q_ref[0, kh * half:(kh + 1) * half, :])
            qs_ref[kh * per_kh:kh * per_kh + half, :] = qa
            qs_ref[kh * per_kh + half:(kh + 1) * per_kh, :] = qb

    def update(kt, vt, new_tokens):
        s = _dot_nt(qs_ref[...], kt)
        row = lax.broadcasted_iota(jnp.int32, s.shape, 0)
        col = lax.broadcasted_iota(jnp.int32, s.shape, 1)
        keep = (col % DIFF_KVH) == (row // per_kh)
        if new_tokens:
            keep = keep & ((col // DIFF_KVH) <= (row % t_len))
        _online_update(jnp.where(keep, s, NEG_INF), [vt], m_ref, l_ref, acc_ref)

    update(kbuf[slot].astype(BF16), vbuf[slot].astype(BF16), False)

    @pl.when(c == pl.num_programs(1) - 1)
    def _():
        update(knew_ref[0], vnew_ref[0], True)
        lam = _diff_lambda(lq1_ref, lk1_ref, lq2_ref, lk2_ref, lam_init)
        for kh in range(DIFF_KVH):
            r1 = slice(kh * per_kh, kh * per_kh + half)
            r2 = slice(kh * per_kh + half, (kh + 1) * per_kh)
            o = acc_ref[r1, :] / l_ref[r1, :] - lam * (acc_ref[r2, :] / l_ref[r2, :])
            o = _rms(o, ng_ref[...]) * (1.0 - lam_init)
            o_ref[0, kh * half:(kh + 1) * half, :] = o.astype(BF16)


def _decode_diff(page_table, q_rows, k_new, v_new, lam_vecs, norm_g, cache_k, cache_v,
                 layer, t_len, lam_init, pps):
    db, rows, _ = q_rows.shape
    n_pages = page_table.shape[1]
    prow = DIFF_KVH * PAGE
    const = lambda shape: pl.BlockSpec(shape, lambda b, c, pt_ref: (0,) * len(shape))
    per_b = lambda shape: pl.BlockSpec(shape, lambda b, c, pt_ref: (b, 0, 0))
    hbm = pl.BlockSpec(memory_space=pl.ANY)
    grid_spec = pltpu.PrefetchScalarGridSpec(
        num_scalar_prefetch=1,
        grid=(db, n_pages // pps),
        in_specs=[per_b((1, rows, LANE)), per_b((1, prow, LANE)), per_b((1, prow, LANE)),
                  const((1, DIFF_HD)), const((1, DIFF_HD)), const((1, DIFF_HD)), const((1, DIFF_HD)),
                  const((1, LANE)), hbm, hbm],
        out_specs=per_b((1, rows, LANE)),
        scratch_shapes=[pltpu.VMEM((2, pps * prow, LANE), F32), pltpu.VMEM((2, pps * prow, LANE), F32),
                        pltpu.SemaphoreType.DMA((2, 2)),
                        pltpu.VMEM((2 * rows, LANE), BF16), pltpu.VMEM((2 * rows, 1), F32),
                        pltpu.VMEM((2 * rows, 1), F32), pltpu.VMEM((2 * rows, LANE), F32)],
    )
    return pl.pallas_call(
        functools.partial(_decode_diff_kernel, pps=pps, n_pages=n_pages, layer=layer, t_len=t_len,
                          lam_init=lam_init),
        grid_spec=grid_spec,
        out_shape=jax.ShapeDtypeStruct((db, rows, LANE), BF16),
        compiler_params=_cparams(("arbitrary", "arbitrary"), 56),
        name="decode_diff",
    )(page_table.reshape(-1), q_rows, k_new, v_new, *lam_vecs, norm_g, cache_k, cache_v)


def _odd_in_kernel(x_ref, w_ref, q_ref, k_ref, v_ref, kb_ref, vb_ref):
    h = _dot(x_ref[...].astype(BF16), w_ref[...])
    nq = DIFF_HEADS * 2 * DIFF_HD
    nkv = DIFF_KVH * 2 * DIFF_HD
    q_ref[...] = h[:, :nq].astype(BF16)
    k = h[:, nq:nq + nkv]
    v = h[:, nq + nkv:]
    k_ref[...] = k
    v_ref[...] = v
    kb_ref[...] = k.astype(BF16)
    vb_ref[...] = v.astype(BF16)


def _odd_in(x2d, w_in, tm):
    m = x2d.shape[0]
    nq = DIFF_HEADS * 2 * DIFF_HD
    nkv = DIFF_KVH * 2 * DIFF_HD
    row = lambda w: pl.BlockSpec((tm, w), lambda i: (i, 0))
    return pl.pallas_call(
        _odd_in_kernel,
        grid=(m // tm,),
        in_specs=[row(D_MODEL), _const_spec((D_MODEL, nq + 2 * nkv))],
        out_specs=[row(nq), row(nkv), row(nkv), row(nkv), row(nkv)],
        out_shape=[jax.ShapeDtypeStruct((m, nq), BF16),
                   jax.ShapeDtypeStruct((m, nkv), F32), jax.ShapeDtypeStruct((m, nkv), F32),
                   jax.ShapeDtypeStruct((m, nkv), BF16), jax.ShapeDtypeStruct((m, nkv), BF16)],
        compiler_params=_cparams(("arbitrary",), 40),
        name="odd_in",
    )(x2d, w_in)


def _post_kernel(*refs, even):
    if even:
        (x_ref, a_ref, pool_ref, wuv_ref, wmix_ref, g1_ref, b1_ref, g2_ref, b2_ref,
         w1_ref, fb1_ref, w2_ref, fb2_ref, o_ref) = refs
        o = _dot(a_ref[...], wuv_ref[...]).astype(BF16)
        mix = _dot(pool_ref[...], wmix_ref[:POOL_DIM, :]) + _dot(o, wmix_ref[POOL_DIM:, :])
    else:
        (x_ref, a_ref, wmix_ref, g1_ref, b1_ref, g2_ref, b2_ref,
         w1_ref, fb1_ref, w2_ref, fb2_ref, o_ref) = refs
        mix = _dot(a_ref[...], wmix_ref[...])
    x1 = _ln(ALPHA * x_ref[...] + mix, g1_ref[...], b1_ref[...])
    x1b = x1.astype(BF16)
    ffn = None
    for c in range(D_FF // FF_CHUNK):
        cols = slice(c * FF_CHUNK, (c + 1) * FF_CHUNK)
        hdn = jnp.square(jnp.maximum(_dot(x1b, w1_ref[:, cols]) + fb1_ref[:, cols], 0.0))
        part = _dot(hdn.astype(BF16), w2_ref[cols, :])
        ffn = part if ffn is None else ffn + part
    o_ref[...] = _ln(ALPHA * x1 + (ffn + fb2_ref[...]), g2_ref[...], b2_ref[...])


def _post(x2d, a, extra, wts, even, tm):
    m = x2d.shape[0]
    row = lambda w: pl.BlockSpec((tm, w), lambda i: (i, 0))
    vec = _const_spec((1, D_MODEL))
    ins = [x2d, a]
    specs = [row(D_MODEL), row(a.shape[1])]
    if even:
        ins += [extra, wts["w_uv"]]
        specs += [row(POOL_DIM), _const_spec((MLA_HEADS * LANE, MLA_HEADS * MLA_V))]
    ins += [wts["w_mix"], wts["g1"], wts["b1"], wts["g2"], wts["b2"],
            wts["w1"], wts["fb1"], wts["w2"], wts["fb2"]]
    specs += [_const_spec((D_MODEL, D_MODEL)), vec, vec, vec, vec,
              _const_spec((D_MODEL, D_FF)), _const_spec((1, D_FF)), _const_spec((D_FF, D_MODEL)), vec]
    return pl.pallas_call(
        functools.partial(_post_kernel, even=even),
        grid=(m // tm,),
        in_specs=specs,
        out_specs=row(D_MODEL),
        out_shape=jax.ShapeDtypeStruct((m, D_MODEL), F32),
        compiler_params=_cparams(("arbitrary",), 56),
        name="post_even" if even else "post_odd",
    )(*ins)


def _rope_tables(pos):
    half = MLA_ROPE // 2
    freq = ROPE_THETA ** (-jnp.arange(half, dtype=F32) / half)
    ang = pos[:, None] * freq[None, :]
    cos, sin = jnp.cos(ang), jnp.sin(ang)
    cc = jnp.concatenate([cos, cos], -1)
    ss = jnp.concatenate([-sin, sin], -1)
    pad = jnp.zeros((pos.shape[0], LANE - MLA_ROPE), F32)
    return (jnp.tile(cc, (1, MLA_HEADS)), jnp.tile(ss, (1, MLA_HEADS)),
            jnp.concatenate([cc, pad], -1), jnp.concatenate([ss, pad], -1))


def _block_diag(blocks):
    h, r, c = blocks.shape
    eye = jnp.eye(h, dtype=blocks.dtype)
    return (blocks[:, :, None, :] * eye[:, None, :, None]).reshape(h * r, h * c)


def _even_weights(w_in, q_norm, w_uq, kv_norm, w_uk, w_uv, w_out):
    half = MLA_ROPE // 2
    kr0 = POOL_DIM + MLA_Q_RANK + MLA_KV_RANK
    zpad = jnp.zeros((D_MODEL, LANE - MLA_ROPE), F32)
    w_in_p = jnp.concatenate(
        [w_in, zpad, w_in[:, kr0 + half:], w_in[:, kr0:kr0 + half], zpad], -1).astype(BF16)
    uq = w_uq.reshape(MLA_Q_RANK, MLA_HEADS, MLA_NOPE + MLA_ROPE)
    nope = uq[:, :, :MLA_NOPE].reshape(MLA_Q_RANK, MLA_HEADS * MLA_NOPE)
    rope = uq[:, :, MLA_NOPE:]
    rope_a = rope.reshape(MLA_Q_RANK, MLA_HEADS * MLA_ROPE)
    rope_b = jnp.concatenate([rope[:, :, half:], rope[:, :, :half]], -1).reshape(MLA_Q_RANK, MLA_HEADS * MLA_ROPE)
    place = jnp.concatenate([jnp.eye(MLA_ROPE, dtype=F32), jnp.zeros((MLA_ROPE, LANE - MLA_ROPE), F32)], -1)
    return {
        "w_in": w_in_p,
        "q_norm": q_norm.reshape(1, -1),
        "w_uq": jnp.concatenate([nope, rope_a, rope_b], -1).astype(BF16),
        "kv_norm": kv_norm.reshape(1, -1),
        "w_bd": _block_diag(jnp.transpose(w_uk, (1, 2, 0))).astype(BF16),
        "place": _block_diag(jnp.broadcast_to(place, (MLA_HEADS,) + place.shape)).astype(BF16),
        "w_uv": _block_diag(jnp.transpose(w_uv, (1, 0, 2))).astype(BF16),
        "w_mix": w_out.astype(BF16),
    }


def _ffn_weights(g1, b1, g2, b2, w1, fb1, w2, fb2):
    r = lambda v: v.reshape(1, -1)
    return {"g1": r(g1), "b1": r(b1), "g2": r(g2), "b2": r(b2),
            "w1": w1.astype(BF16), "fb1": r(fb1), "w2": w2.astype(BF16), "fb2": r(fb2)}


def _token_tile(m):
    return TOKEN_TILE if m % TOKEN_TILE == 0 else m


def kernel(x_prompt, x_sample, cache_mla_ckv, cache_mla_krope, cache_diff_k, cache_diff_v, state_pool,
           page_table, w_in_even, pool_w, pool_scale, mla_q_norm, mla_w_uq, mla_kv_norm, mla_w_uk,
           mla_w_uv, w_out_even, w_in_odd, diff_lq1, diff_lk1, diff_lq2, diff_lk2, diff_norm, w_out_odd,
           ln1_g, ln1_b, ln2_g, ln2_b, mlp_w1, mlp_b1, mlp_w2, mlp_b2):
    bsz, s_len, d = x_prompt.shape
    db, t_len, _ = x_sample.shape
    n_pages = page_table.shape[1]
    past = n_pages * PAGE
    mp, ms = bsz * s_len, db * t_len
    tmp, tms = _token_tile(mp), _token_tile(ms)
    tq = min(FLASH_TQ, s_len)
    pps_mla = min(MLA_PAGES_PER_STEP, n_pages)
    pps_diff = min(DIFF_PAGES_PER_STEP, n_pages)
    n_pool = cache_diff_k.shape[1]
    kvw = DIFF_KVH * 2 * DIFF_HD
    cache_dk = cache_diff_k.reshape(cache_diff_k.shape[0], n_pool, DIFF_KVH * PAGE, LANE)
    cache_dv = cache_diff_v.reshape(cache_diff_v.shape[0], n_pool, DIFF_KVH * PAGE, LANE)
    cache_krt = jnp.swapaxes(cache_mla_krope, 2, 3)

    tabs_p = _rope_tables(jnp.arange(s_len, dtype=F32))
    tabs_s = tuple(jnp.tile(t, (tms // t_len, 1)) for t in _rope_tables(past + jnp.arange(t_len, dtype=F32)))
    pad_rows = lambda a, n: jnp.pad(a, ((0, 0), (0, n - a.shape[1]), (0, 0)))
    hist = 24 - t_len

    xp = x_prompt.reshape(mp, d)
    xs = x_sample.reshape(ms, d)
    outs = {k: [] for k in ("p_ckv", "p_kr", "p_dk", "p_dv", "p_pool", "s_ckv", "s_kr", "s_dk", "s_dv", "s_pool")}
    for i in range(DEPTH):
        ffn = _ffn_weights(ln1_g[i], ln1_b[i], ln2_g[i], ln2_b[i], mlp_w1[i], mlp_b1[i], mlp_w2[i], mlp_b2[i])
        if i % 2 == 0:
            e = i // 2
            wts = _even_weights(w_in_even[e], mla_q_norm[e], mla_w_uq[e], mla_kv_norm[e], mla_w_uk[e],
                                mla_w_uv[e], w_out_even[e])
            wts.update(ffn)
            pw = pool_w[e].astype(BF16)
            psc = pool_scale[e].reshape(1, -1)
            u, qcat, kcat, ckv, kr = _even_in(xp, wts, tabs_p, s_len // tmp if s_len % tmp == 0 else 1, tmp)
            u3 = u.reshape(bsz, s_len, POOL_DIM)
            pool_o = _pool(u3, pw, psc, 0, 1).reshape(mp, POOL_DIM)
            kc3 = kcat.reshape(bsz, s_len, QCAT)
            vt = jnp.transpose(kc3[:, :, :LANE].reshape(bsz, s_len // tq, tq, LANE), (0, 1, 3, 2))
            o_lat = _flash_mla(qcat.reshape(bsz, s_len, -1), kc3, vt, tq)
            xp = _post(xp, o_lat.reshape(mp, -1), pool_o, wts, True, tmp)
            outs["p_ckv"].append(ckv.reshape(bsz, s_len, -1))
            outs["p_kr"].append(kr.reshape(bsz, s_len, -1))
            outs["p_pool"].append(u3[:, s_len - POOL_STATE:])
            u, qcat, kcat, ckv, kr = _even_in(xs, wts, tabs_s, 1, tms)
            u3 = u.reshape(db, t_len, POOL_DIM)
            u_ext = jnp.concatenate([state_pool[e], u3], axis=1)
            u_pad = jnp.pad(u_ext, ((0, 0), (24 - u_ext.shape[1], 0), (0, 0)))
            bb = db
            while bb * u_pad.shape[1] > 2048 and bb % 2 == 0:
                bb //= 2
            pool_o = _pool(u_pad, pw, psc, past - hist, bb)[:, hist:].reshape(ms, POOL_DIM)
            q_rows = jnp.transpose(qcat.reshape(db, t_len, MLA_HEADS, QCAT), (0, 2, 1, 3))
            o_rows = _decode_mla(page_table, q_rows.reshape(db, MLA_HEADS * t_len, QCAT),
                                 pad_rows(kcat.reshape(db, t_len, QCAT), PAGE),
                                 cache_mla_ckv, cache_krt, e, t_len, pps_mla)
            o_lat = jnp.transpose(o_rows.reshape(db, MLA_HEADS, t_len, LANE), (0, 2, 1, 3)).reshape(ms, -1)
            xs = _post(xs, o_lat, pool_o, wts, True, tms)
            outs["s_ckv"].append(ckv.reshape(db, t_len, -1))
            outs["s_kr"].append(kr.reshape(db, t_len, -1))
            outs["s_pool"].append(u_ext[:, u_ext.shape[1] - POOL_STATE:])
        else:
            o = i // 2
            lam_init = 0.8 - 0.6 * math.exp(-0.3 * i)
            lam_vecs = tuple(v[o].reshape(1, -1) for v in (diff_lq1, diff_lk1, diff_lq2, diff_lk2))
            norm_g = diff_norm[o].reshape(1, -1)
            wts = dict(ffn)
            wts["w_mix"] = w_out_odd[o].astype(BF16)
            w_in = w_in_odd[o].astype(BF16)
            q, k, v, kb, vb = _odd_in(xp, w_in, tmp)
            vt = jnp.transpose(vb.reshape(bsz, s_len // tq, tq, DIFF_KVH, LANE), (0, 3, 1, 4, 2))
            att = _flash_diff(q.reshape(bsz, s_len, -1), kb.reshape(bsz, s_len, kvw), vt,
                              lam_vecs, norm_g.reshape(-1, 1), lam_init, tq)
            xp = _post(xp, att.reshape(mp, -1), None, wts, False, tmp)
            outs["p_dk"].append(k.reshape(bsz, s_len, DIFF_KVH, 2 * DIFF_HD))
            outs["p_dv"].append(v.reshape(bsz, s_len, DIFF_KVH, 2 * DIFF_HD))
            q, k, v, kb, vb = _odd_in(xs, w_in, tms)
            heads = DIFF_KVH * DIFF_GROUP
            q_rows = jnp.transpose(q.reshape(db, t_len, heads, LANE), (0, 2, 1, 3)).reshape(db, heads * t_len, LANE)
            new_rows = lambda a: pad_rows(a.reshape(db, t_len * DIFF_KVH, LANE), DIFF_KVH * PAGE)
            o_rows = _decode_diff(page_table, q_rows, new_rows(kb), new_rows(vb), lam_vecs, norm_g,
                                  cache_dk, cache_dv, o, t_len, lam_init, pps_diff)
            att = jnp.transpose(o_rows.reshape(db, heads, t_len, LANE), (0, 2, 1, 3)).reshape(ms, -1)
            xs = _post(xs, att, None, wts, False, tms)
            outs["s_dk"].append(k.reshape(db, t_len, DIFF_KVH, 2 * DIFF_HD))
            outs["s_dv"].append(v.reshape(db, t_len, DIFF_KVH, 2 * DIFF_HD))

    st = lambda name: jnp.stack(outs[name])
    return (xp.reshape(bsz, s_len, d), xs.reshape(db, t_len, d),
            st("p_ckv"), st("p_kr"), st("p_dk"), st("p_dv"), st("p_pool"),
            st("s_ckv"), st("s_kr"), st("s_dk"), st("s_dv"), st("s_pool"))
```

```python
import functools
import math

import jax
import jax.numpy as jnp
from jax import lax
from jax.experimental import pallas as pl
from jax.experimental.pallas import tpu as pltpu

F32 = jnp.float32
BF16 = jnp.bfloat16

D_MODEL = 1024
DEPTH = 4
PAGE = 128
POOL_DIM = 512
POOL_WINDOWS = (2, 4, 8, 16)
POOL_GD = 128
POOL_PAD = 16
POOL_STATE = 15
MLA_HEADS = 8
MLA_NOPE = 64
MLA_ROPE = 32
MLA_V = 64
MLA_Q_RANK = 256
MLA_KV_RANK = 128
MLA_SCALE = (MLA_NOPE + MLA_ROPE) ** -0.5
LOG2E = math.log2(math.e)
ROPE_THETA = 10000.0
DIFF_HEADS = 8
DIFF_HD = 64
DIFF_KVH = 2
DIFF_GROUP = 4
DIFF_SCALE = DIFF_HD ** -0.5
D_FF = 4096
ALPHA = (2 * DEPTH) ** 0.25
EPS = 1e-5
QCAT = 256
EVEN_IN_P = 1152

LANE = 128
TOKEN_TILE = 512
FLASH_TQ = 512
FF_CHUNK = 1024
MLA_PAGES_PER_STEP = 64
DIFF_PAGES_PER_STEP = 32
NEG_INF = float("-inf")


def _cparams(sem, vmem_mib):
    return pltpu.CompilerParams(dimension_semantics=sem, vmem_limit_bytes=vmem_mib * 1024 * 1024)


def _const_spec(shape):
    nd = len(shape)
    return pl.BlockSpec(shape, lambda *_: (0,) * nd, pipeline_mode=pl.Buffered(1))


def _dot(a, b):
    return jnp.dot(a, b, preferred_element_type=F32)


def _dot_nt(a, b):
    return lax.dot_general(a, b, (((1,), (1,)), ((), ())), preferred_element_type=F32)


def _rms(x, g):
    return x * lax.rsqrt(jnp.mean(x * x, -1, keepdims=True) + EPS) * g


def _ln(x, g, b):
    mu = jnp.mean(x, -1, keepdims=True)
    xc = x - mu
    var = jnp.mean(xc * xc, -1, keepdims=True)
    return xc * lax.rsqrt(var + EPS) * g + b


def _even_in_kernel(x_ref, w_in_ref, qn_ref, wuq_ref, kvn_ref, wbd_ref, place_ref,
                    cq_ref, sq_ref, ck_ref, sk_ref,
                    u_ref, qcat_ref, kcat_ref, ckv_ref, kr_ref):
    h = _dot(x_ref[...].astype(BF16), w_in_ref[...])
    u_ref[...] = h[:, :POOL_DIM]
    cqn = _rms(h[:, 512:768], qn_ref[...])
    q = _dot(cqn.astype(BF16), wuq_ref[...])
    rot = (q[:, 512:768] * cq_ref[...] + q[:, 768:1024] * sq_ref[...]) * (MLA_SCALE * LOG2E)
    qlat = _dot(q[:, :512].astype(BF16), wbd_ref[...]) * (MLA_SCALE * LOG2E)
    qrope = _dot(rot.astype(BF16), place_ref[...])
    for hh in range(MLA_HEADS):
        qcat_ref[:, hh * QCAT:hh * QCAT + LANE] = qlat[:, hh * LANE:(hh + 1) * LANE].astype(BF16)
        qcat_ref[:, hh * QCAT + LANE:(hh + 1) * QCAT] = qrope[:, hh * LANE:(hh + 1) * LANE].astype(BF16)
    ckvn = _rms(h[:, 768:896], kvn_ref[...])
    ckv_ref[...] = ckvn
    krr = h[:, 896:1024] * ck_ref[...] + h[:, 1024:1152] * sk_ref[...]
    kr_ref[...] = krr[:, :MLA_ROPE]
    kcat_ref[:, :LANE] = ckvn.astype(BF16)
    kcat_ref[:, LANE:] = krr.astype(BF16)


def _even_in(x2d, wts, tabs, n_tab_blocks, tm):
    m = x2d.shape[0]
    row = lambda w: pl.BlockSpec((tm, w), lambda i: (i, 0))
    tab = lambda w: pl.BlockSpec((tm, w), lambda i: (i % n_tab_blocks, 0))
    return pl.pallas_call(
        _even_in_kernel,
        grid=(m // tm,),
        in_specs=[row(D_MODEL), _const_spec((D_MODEL, EVEN_IN_P)), _const_spec((1, MLA_Q_RANK)),
                  _const_spec((MLA_Q_RANK, 1024)), _const_spec((1, MLA_KV_RANK)),
                  _const_spec((512, 1024)), _const_spec((256, 1024)),
                  tab(256), tab(256), tab(LANE), tab(LANE)],
        out_specs=[row(POOL_DIM), row(MLA_HEADS * QCAT), row(QCAT), row(MLA_KV_RANK), row(MLA_ROPE)],
        out_shape=[jax.ShapeDtypeStruct((m, POOL_DIM), F32),
                   jax.ShapeDtypeStruct((m, MLA_HEADS * QCAT), BF16),
                   jax.ShapeDtypeStruct((m, QCAT), BF16),
                   jax.ShapeDtypeStruct((m, MLA_KV_RANK), F32),
                   jax.ShapeDtypeStruct((m, MLA_ROPE), F32)],
        compiler_params=_cparams(("arbitrary",), 40),
        name="even_in",
    )(x2d, wts["w_in"], wts["q_norm"], wts["w_uq"], wts["kv_norm"], wts["w_bd"], wts["place"], *tabs)


def _pool_kernel(u_ref, w_ref, sc_ref, o_ref, ext_ref, *, start_pos):
    bb, seq, _ = u_ref.shape
    ext_ref[:, :POOL_PAD, :] = jnp.zeros((bb, POOL_PAD, POOL_DIM), F32)
    ext_ref[:, POOL_PAD:, :] = u_ref[...]
    pos = start_pos + lax.broadcasted_iota(jnp.int32, (1, seq, 1), 1).astype(F32)
    for g, win in enumerate(POOL_WINDOWS):
        lanes = slice(g * POOL_GD, (g + 1) * POOL_GD)
        tot = ext_ref[:, POOL_PAD:POOL_PAD + seq, lanes]
        for j in range(1, win):
            tot = tot + ext_ref[:, POOL_PAD - j:POOL_PAD - j + seq, lanes]
        cnt = jnp.minimum(pos + 1.0, float(win))
        d = tot / cnt - u_ref[:, :, lanes]
        y = _dot(d.reshape(bb * seq, POOL_GD).astype(BF16), w_ref[g])
        o_ref[:, :, lanes] = (y.reshape(bb, seq, POOL_GD) * sc_ref[:, lanes]).astype(BF16)


def _pool(u_ext, pool_w, pool_scale, start_pos, bb):
    nb, seq, _ = u_ext.shape
    blk = pl.BlockSpec((bb, seq, POOL_DIM), lambda i: (i, 0, 0))
    return pl.pallas_call(
        functools.partial(_pool_kernel, start_pos=float(start_pos)),
        grid=(nb // bb,),
        in_specs=[blk, _const_spec((len(POOL_WINDOWS), POOL_GD, POOL_GD)), _const_spec((1, POOL_DIM))],
        out_specs=blk,
        out_shape=jax.ShapeDtypeStruct((nb, seq, POOL_DIM), BF16),
        scratch_shapes=[pltpu.VMEM((bb, POOL_PAD + seq, POOL_DIM), F32)],
        compiler_params=_cparams(("arbitrary",), 40),
        name="pool_mix",
    )(u_ext, pool_w, pool_scale)


def _flash_sweep(qs_ref, k_ref, vt_ref, m_ref, l_ref, acc_ref, st_ref, *, q_idx, tq, groups):
    rows = groups * tq
    m_ref[...] = jnp.full((1, rows), NEG_INF, F32)
    l_ref[...] = jnp.zeros((1, rows), F32)
    acc_ref[...] = jnp.zeros((LANE, rows), F32)
    q = qs_ref[...]

    def scores(j, diagonal):
        start = pl.multiple_of(j * tq, tq)
        st = _dot_nt(k_ref[pl.ds(start, tq), :], q)
        if diagonal:
            key = lax.broadcasted_iota(jnp.int32, (tq, rows), 0)
            t = lax.broadcasted_iota(jnp.int32, (tq, rows), 1) % tq
            st = jnp.where(key <= t, st, NEG_INF)
        return st

    def consume(j, st):
        m_old = m_ref[...]
        m_new = jnp.maximum(m_old, jnp.max(st, 0, keepdims=True))
        p = jnp.exp2(st - m_new)
        alpha = jnp.exp2(m_old - m_new)
        l_ref[...] = alpha * l_ref[...] + jnp.sum(p, 0, keepdims=True)
        acc_ref[...] = alpha * acc_ref[...] + _dot(vt_ref[j], p.astype(BF16))
        m_ref[...] = m_new

    @pl.when(q_idx == 0)
    def _():
        consume(0, scores(0, True))

    @pl.when(q_idx > 0)
    def _():
        st_ref[...] = scores(0, False)

        def body(j, carry):
            nxt = scores(j + 1, False)
            consume(j, st_ref[...])
            st_ref[...] = nxt
            return carry

        lax.fori_loop(0, q_idx - 1, body, 0)
        nxt = scores(q_idx, True)
        consume(q_idx - 1, st_ref[...])
        consume(q_idx, nxt)


def _flash_scratch(rows, dk, tq):
    return [pltpu.VMEM((rows, dk), BF16), pltpu.VMEM((1, rows), F32), pltpu.VMEM((1, rows), F32),
            pltpu.VMEM((LANE, rows), F32), pltpu.VMEM((tq, rows), F32)]


def _flash_mla_kernel(q_ref, k_ref, vt_ref, o_ref, qs_ref, m_ref, l_ref, acc_ref, st_ref, *, tq):
    for hh in range(MLA_HEADS):
        qs_ref[hh * tq:(hh + 1) * tq, :] = q_ref[0, :, hh * QCAT:(hh + 1) * QCAT]
    _flash_sweep(qs_ref, k_ref.at[0], vt_ref.at[0], m_ref, l_ref, acc_ref, st_ref,
                 q_idx=pl.program_id(1), tq=tq, groups=MLA_HEADS)
    for hh in range(MLA_HEADS):
        cols = slice(hh * tq, (hh + 1) * tq)
        o = acc_ref[:, cols] * (1.0 / l_ref[:, cols])
        o_ref[0, :, hh * LANE:(hh + 1) * LANE] = o.T.astype(BF16)


def _flash_mla(qcat, kcat, vt, tq):
    b, s, _ = qcat.shape
    return pl.pallas_call(
        functools.partial(_flash_mla_kernel, tq=tq),
        grid=(b, s // tq),
        in_specs=[pl.BlockSpec((1, tq, MLA_HEADS * QCAT), lambda bi, i: (bi, i, 0)),
                  pl.BlockSpec((1, s, QCAT), lambda bi, i: (bi, 0, 0)),
                  pl.BlockSpec((1, s // tq, LANE, tq), lambda bi, i: (bi, 0, 0, 0))],
        out_specs=pl.BlockSpec((1, tq, MLA_HEADS * LANE), lambda bi, i: (bi, i, 0)),
        out_shape=jax.ShapeDtypeStruct((b, s, MLA_HEADS * LANE), BF16),
        scratch_shapes=_flash_scratch(MLA_HEADS * tq, QCAT, tq),
        compiler_params=_cparams(("arbitrary", "arbitrary"), 60),
        name="flash_mla",
    )(qcat, kcat, vt)


def _diff_lambda(lq1_ref, lk1_ref, lq2_ref, lk2_ref, lam_init):
    a = jnp.sum(lq1_ref[...] * lk1_ref[...], -1, keepdims=True)
    b = jnp.sum(lq2_ref[...] * lk2_ref[...], -1, keepdims=True)
    return jnp.exp(a) - jnp.exp(b) + lam_init


def _diff_mask_rows(tile):
    lane = lax.broadcasted_iota(jnp.int32, tile.shape, 1)
    zero = jnp.zeros_like(tile)
    return jnp.where(lane < DIFF_HD, tile, zero), jnp.where(lane >= DIFF_HD, tile, zero)


def _flash_diff_kernel(q_ref, k_ref, vt_ref, lq1_ref, lk1_ref, lq2_ref, lk2_ref, ngc_ref,
                       o_ref, qs_ref, m_ref, l_ref, acc_ref, st_ref, *, tq, lam_init):
    for g in range(DIFF_GROUP):
        qa, qb = _diff_mask_rows(q_ref[0, :, g * LANE:(g + 1) * LANE])
        qs_ref[(2 * g) * tq:(2 * g + 1) * tq, :] = qa
        qs_ref[(2 * g + 1) * tq:(2 * g + 2) * tq, :] = qb
    _flash_sweep(qs_ref, k_ref.at[0], vt_ref.at[0, 0], m_ref, l_ref, acc_ref, st_ref,
                 q_idx=pl.program_id(2), tq=tq, groups=2 * DIFF_GROUP)
    lam = _diff_lambda(lq1_ref, lk1_ref, lq2_ref, lk2_ref, lam_init)
    for g in range(DIFF_GROUP):
        c1 = slice((2 * g) * tq, (2 * g + 1) * tq)
        c2 = slice((2 * g + 1) * tq, (2 * g + 2) * tq)
        o = acc_ref[:, c1] * (1.0 / l_ref[:, c1]) - acc_ref[:, c2] * (lam / l_ref[:, c2])
        o = o * lax.rsqrt(jnp.mean(o * o, 0, keepdims=True) + EPS) * ngc_ref[...] * (1.0 - lam_init)
        o_ref[0, :, g * LANE:(g + 1) * LANE] = o.T.astype(BF16)


def _flash_diff(q, k, vt, lam_vecs, norm_col, lam_init, tq):
    b, s, _ = q.shape
    gw = DIFF_GROUP * LANE
    vec = _const_spec((1, DIFF_HD))
    return pl.pallas_call(
        functools.partial(_flash_diff_kernel, tq=tq, lam_init=lam_init),
        grid=(b, DIFF_KVH, s // tq),
        in_specs=[pl.BlockSpec((1, tq, gw), lambda bi, kh, i: (bi, i, kh)),
                  pl.BlockSpec((1, s, LANE), lambda bi, kh, i: (bi, 0, kh)),
                  pl.BlockSpec((1, 1, s // tq, LANE, tq), lambda bi, kh, i: (bi, kh, 0, 0, 0)),
                  vec, vec, vec, vec, _const_spec((LANE, 1))],
        out_specs=pl.BlockSpec((1, tq, gw), lambda bi, kh, i: (bi, i, kh)),
        out_shape=jax.ShapeDtypeStruct((b, s, DIFF_KVH * gw), BF16),
        scratch_shapes=_flash_scratch(2 * DIFF_GROUP * tq, LANE, tq),
        compiler_params=_cparams(("arbitrary", "arbitrary", "arbitrary"), 60),
        name="flash_diff",
    )(q, k, vt, *lam_vecs, norm_col)


def _online_update(s, vals, m_ref, l_ref, acc_ref):
    w = s.shape[1] // len(vals)
    m_old = m_ref[...]
    m_new = jnp.maximum(m_old, jnp.max(s, -1, keepdims=True))
    p = jnp.exp2(s - m_new)
    alpha = jnp.exp2(m_old - m_new)
    l_ref[...] = alpha * l_ref[...] + jnp.sum(p, -1, keepdims=True)
    pv = None
    for j, vj in enumerate(vals):
        t = _dot(p[:, j * w:(j + 1) * w].astype(BF16), vj)
        pv = t if pv is None else pv + t
    acc_ref[...] = alpha * acc_ref[...] + pv
    m_ref[...] = m_new


def _paged_copies(pt_ref, srcs, dsts, sem, b, c, slot, *, pps, n_pages, layer, start):
    base = b * n_pages + c * pps
    for j in range(pps):
        pg = pt_ref[base + j]
        for a, (src, dst) in enumerate(zip(srcs, dsts)):
            cp = pltpu.make_async_copy(src.at[layer, pg], dst(slot, j), sem.at[a, slot])
            if start:
                cp.start()
            else:
                cp.wait()


def _paged_pipeline(fetch):
    b, c = pl.program_id(0), pl.program_id(1)
    nb, nc = pl.num_programs(0), pl.num_programs(1)
    step = b * nc + c
    slot = step % 2

    @pl.when(step == 0)
    def _():
        fetch(b, c, slot, start=True)

    @pl.when(step + 1 < nb * nc)
    def _():
        wrap = c + 1 == nc
        fetch(jnp.where(wrap, b + 1, b), jnp.where(wrap, 0, c + 1), 1 - slot, start=True)

    fetch(b, c, slot, start=False)
    return slot


def _decode_mla_kernel(pt_ref, q_ref, knew_ref, ckv_hbm, krt_hbm, o_ref,
                       kbuf, rbuf, sem, m_ref, l_ref, acc_ref, *, pps, n_pages, layer, t_len):
    c = pl.program_id(1)
    dsts = [lambda s, j: kbuf.at[s, pl.ds(j * PAGE, PAGE), :],
            lambda s, j: rbuf.at[s, :, pl.ds(j * PAGE, PAGE)]]
    slot = _paged_pipeline(functools.partial(_paged_copies, pt_ref, [ckv_hbm, krt_hbm], dsts, sem,
                                             pps=pps, n_pages=n_pages, layer=layer))

    @pl.when(c == 0)
    def _():
        m_ref[...] = jnp.full(m_ref.shape, NEG_INF, F32)
        l_ref[...] = jnp.zeros(l_ref.shape, F32)
        acc_ref[...] = jnp.zeros(acc_ref.shape, F32)

    q = q_ref[0]
    kb = kbuf[slot].astype(BF16)
    s = _dot_nt(q[:, :LANE], kb) + _dot(q[:, LANE:LANE + MLA_ROPE], rbuf[slot].astype(BF16))
    _online_update(s, [kb], m_ref, l_ref, acc_ref)

    @pl.when(c == pl.num_programs(1) - 1)
    def _():
        kn = knew_ref[0]
        sn = _dot_nt(q, kn)
        t = lax.broadcasted_iota(jnp.int32, sn.shape, 0) % t_len
        j = lax.broadcasted_iota(jnp.int32, sn.shape, 1)
        _online_update(jnp.where(j <= t, sn, NEG_INF), [kn[:, :LANE]], m_ref, l_ref, acc_ref)
        o_ref[0] = (acc_ref[...] / l_ref[...]).astype(BF16)


def _decode_mla(page_table, q_rows, k_new, cache_ckv, cache_krt, layer, t_len, pps):
    db, rows, _ = q_rows.shape
    n_pages = page_table.shape[1]
    per_b = lambda shape: pl.BlockSpec(shape, lambda b, c, pt_ref: (b, 0, 0))
    hbm = pl.BlockSpec(memory_space=pl.ANY)
    grid_spec = pltpu.PrefetchScalarGridSpec(
        num_scalar_prefetch=1,
        grid=(db, n_pages // pps),
        in_specs=[per_b((1, rows, QCAT)), per_b((1, PAGE, QCAT)), hbm, hbm],
        out_specs=per_b((1, rows, LANE)),
        scratch_shapes=[pltpu.VMEM((2, pps * PAGE, MLA_KV_RANK), F32),
                        pltpu.VMEM((2, MLA_ROPE, pps * PAGE), F32),
                        pltpu.SemaphoreType.DMA((2, 2)),
                        pltpu.VMEM((rows, 1), F32), pltpu.VMEM((rows, 1), F32),
                        pltpu.VMEM((rows, LANE), F32)],
    )
    return pl.pallas_call(
        functools.partial(_decode_mla_kernel, pps=pps, n_pages=n_pages, layer=layer, t_len=t_len),
        grid_spec=grid_spec,
        out_shape=jax.ShapeDtypeStruct((db, rows, LANE), BF16),
        compiler_params=_cparams(("arbitrary", "arbitrary"), 40),
        name="decode_mla",
    )(page_table.reshape(-1), q_rows, k_new, cache_ckv, cache_krt)


def _decode_diff_kernel(pt_ref, q_ref, knew_ref, vnew_ref, lq1_ref, lk1_ref, lq2_ref, lk2_ref, ng_ref,
                        k_hbm, v_hbm, o_ref, kbuf, vbuf, sem, qs_ref, m_ref, l_ref, acc_ref,
                        *, pps, n_pages, layer, t_len, lam_init):
    c = pl.program_id(1)
    half = DIFF_GROUP * t_len
    per_kh = 2 * half
    prow = DIFF_KVH * PAGE
    dsts = [lambda s, j: kbuf.at[s, pl.ds(j * prow, prow), :],
            lambda s, j: vbuf.at[s, pl.ds(j * prow, prow), :]]
    slot = _paged_pipeline(functools.partial(_paged_copies, pt_ref, [k_hbm, v_hbm], dsts, sem,
                                             pps=pps, n_pages=n_pages, layer=layer))

    @pl.when(c == 0)
    def _():
        m_ref[...] = jnp.full(m_ref.shape, NEG_INF, F32)
        l_ref[...] = jnp.zeros(l_ref.shape, F32)
        acc_ref[...] = jnp.zeros(acc_ref.shape, F32)
        for kh in range(DIFF_KVH):
            qa, qb = _diff_mask_rows(q_ref[0, kh * half:(kh + 1) * half, :])
            qs_ref[kh * per_kh:kh * per_kh + half, :] = qa
            qs_ref[kh * per_kh + half:(kh + 1) * per_kh, :] = qb

    def update(kt, vt, new_tokens):
        s = _dot_nt(qs_ref[...], kt)
        row = lax.broadcasted_iota(jnp.int32, s.shape, 0)
        col = lax.broadcasted_iota(jnp.int32, s.shape, 1)
        keep = (col % DIFF_KVH) == (row // per_kh)
        if new_tokens:
            keep = keep & ((col // DIFF_KVH) <= (row % t_len))
        _online_update(jnp.where(keep, s, NEG_INF), [vt], m_ref, l_ref, acc_ref)

    update(kbuf[slot].astype(BF16), vbuf[slot].astype(BF16), False)

    @pl.when(c == pl.num_programs(1) - 1)
    def _():
        update(knew_ref[0], vnew_ref[0], True)
        lam = _diff_lambda(lq1_ref, lk1_ref, lq2_ref, lk2_ref, lam_init)
        for kh in range(DIFF_KVH):
            r1 = slice(kh * per_kh, kh * per_kh + half)
            r2 = slice(kh * per_kh + half, (kh + 1) * per_kh)
            o = acc_ref[r1, :] / l_ref[r1, :] - lam * (acc_ref[r2, :] / l_ref[r2, :])
            o = _rms(o, ng_ref[...]) * (1.0 - lam_init)
            o_ref[0, kh * half:(kh + 1) * half, :] = o.astype(BF16)


def _decode_diff(page_table, q_rows, k_new, v_new, lam_vecs, norm_g, cache_k, cache_v,
                 layer, t_len, lam_init, pps):
    db, rows, _ = q_rows.shape
    n_pages = page_table.shape[1]
    prow = DIFF_KVH * PAGE
    const = lambda shape: pl.BlockSpec(shape, lambda b, c, pt_ref: (0,) * len(shape))
    per_b = lambda shape: pl.BlockSpec(shape, lambda b, c, pt_ref: (b, 0, 0))
    hbm = pl.BlockSpec(memory_space=pl.ANY)
    grid_spec = pltpu.PrefetchScalarGridSpec(
        num_scalar_prefetch=1,
        grid=(db, n_pages // pps),
        in_specs=[per_b((1, rows, LANE)), per_b((1, prow, LANE)), per_b((1, prow, LANE)),
                  const((1, DIFF_HD)), const((1, DIFF_HD)), const((1, DIFF_HD)), const((1, DIFF_HD)),
                  const((1, LANE)), hbm, hbm],
        out_specs=per_b((1, rows, LANE)),
        scratch_shapes=[pltpu.VMEM((2, pps * prow, LANE), F32), pltpu.VMEM((2, pps * prow, LANE), F32),
                        pltpu.SemaphoreType.DMA((2, 2)),
                        pltpu.VMEM((2 * rows, LANE), BF16), pltpu.VMEM((2 * rows, 1), F32),
                        pltpu.VMEM((2 * rows, 1), F32), pltpu.VMEM((2 * rows, LANE), F32)],
    )
    return pl.pallas_call(
        functools.partial(_decode_diff_kernel, pps=pps, n_pages=n_pages, layer=layer, t_len=t_len,
                          lam_init=lam_init),
        grid_spec=grid_spec,
        out_shape=jax.ShapeDtypeStruct((db, rows, LANE), BF16),
        compiler_params=_cparams(("arbitrary", "arbitrary"), 56),
        name="decode_diff",
    )(page_table.reshape(-1), q_rows, k_new, v_new, *lam_vecs, norm_g, cache_k, cache_v)


def _odd_in_kernel(x_ref, w_ref, q_ref, k_ref, v_ref, kb_ref, vb_ref):
    h = _dot(x_ref[...].astype(BF16), w_ref[...])
    nq = DIFF_HEADS * 2 * DIFF_HD
    nkv = DIFF_KVH * 2 * DIFF_HD
    q_ref[...] = (h[:, :nq] * (DIFF_SCALE * LOG2E)).astype(BF16)
    k = h[:, nq:nq + nkv]
    v = h[:, nq + nkv:]
    tm = k.shape[0]
    for kh in range(DIFF_KVH):
        k_ref[pl.ds(kh, tm, stride=DIFF_KVH), :] = k[:, kh * LANE:(kh + 1) * LANE]
        v_ref[pl.ds(kh, tm, stride=DIFF_KVH), :] = v[:, kh * LANE:(kh + 1) * LANE]
    kb_ref[...] = k.astype(BF16)
    vb_ref[...] = v.astype(BF16)


def _odd_in(x2d, w_in, tm):
    m = x2d.shape[0]
    nq = DIFF_HEADS * 2 * DIFF_HD
    nkv = DIFF_KVH * 2 * DIFF_HD
    row = lambda w: pl.BlockSpec((tm, w), lambda i: (i, 0))
    return pl.pallas_call(
        _odd_in_kernel,
        grid=(m // tm,),
        in_specs=[row(D_MODEL), _const_spec((D_MODEL, nq + 2 * nkv))],
        out_specs=[row(nq), pl.BlockSpec((DIFF_KVH * tm, LANE), lambda i: (i, 0)),
                   pl.BlockSpec((DIFF_KVH * tm, LANE), lambda i: (i, 0)), row(nkv), row(nkv)],
        out_shape=[jax.ShapeDtypeStruct((m, nq), BF16),
                   jax.ShapeDtypeStruct((DIFF_KVH * m, LANE), F32), jax.ShapeDtypeStruct((DIFF_KVH * m, LANE), F32),
                   jax.ShapeDtypeStruct((m, nkv), BF16), jax.ShapeDtypeStruct((m, nkv), BF16)],
        compiler_params=_cparams(("arbitrary",), 40),
        name="odd_in",
    )(x2d, w_in)


def _post_kernel(*refs, even):
    if even:
        (x_ref, a_ref, pool_ref, wuv_ref, wmix_ref, g1_ref, b1_ref, g2_ref, b2_ref,
         w1_ref, fb1_ref, w2_ref, fb2_ref, o_ref) = refs
        o = _dot(a_ref[...], wuv_ref[...]).astype(BF16)
        mix = _dot(pool_ref[...], wmix_ref[:POOL_DIM, :]) + _dot(o, wmix_ref[POOL_DIM:, :])
    else:
        (x_ref, a_ref, wmix_ref, g1_ref, b1_ref, g2_ref, b2_ref,
         w1_ref, fb1_ref, w2_ref, fb2_ref, o_ref) = refs
        mix = _dot(a_ref[...], wmix_ref[...])
    x1 = _ln(ALPHA * x_ref[...] + mix, g1_ref[...], b1_ref[...])
    x1b = x1.astype(BF16)
    ffn = None
    for c in range(D_FF // FF_CHUNK):
        cols = slice(c * FF_CHUNK, (c + 1) * FF_CHUNK)
        hdn = jnp.square(jnp.maximum(_dot(x1b, w1_ref[:, cols]) + fb1_ref[:, cols], 0.0))
        part = _dot(hdn.astype(BF16), w2_ref[cols, :])
        ffn = part if ffn is None else ffn + part
    o_ref[...] = _ln(ALPHA * x1 + (ffn + fb2_ref[...]), g2_ref[...], b2_ref[...])


def _post(x2d, a, extra, wts, even, tm):
    m = x2d.shape[0]
    row = lambda w: pl.BlockSpec((tm, w), lambda i: (i, 0))
    vec = _const_spec((1, D_MODEL))
    ins = [x2d, a]
    specs = [row(D_MODEL), row(a.shape[1])]
    if even:
        ins += [extra, wts["w_uv"]]
        specs += [row(POOL_DIM), _const_spec((MLA_HEADS * LANE, MLA_HEADS * MLA_V))]
    ins += [wts["w_mix"], wts["g1"], wts["b1"], wts["g2"], wts["b2"],
            wts["w1"], wts["fb1"], wts["w2"], wts["fb2"]]
    specs += [_const_spec((D_MODEL, D_MODEL)), vec, vec, vec, vec,
              _const_spec((D_MODEL, D_FF)), _const_spec((1, D_FF)), _const_spec((D_FF, D_MODEL)), vec]
    return pl.pallas_call(
        functools.partial(_post_kernel, even=even),
        grid=(m // tm,),
        in_specs=specs,
        out_specs=row(D_MODEL),
        out_shape=jax.ShapeDtypeStruct((m, D_MODEL), F32),
        compiler_params=_cparams(("arbitrary",), 56),
        name="post_even" if even else "post_odd",
    )(*ins)


def _rope_tables(pos):
    half = MLA_ROPE // 2
    freq = ROPE_THETA ** (-jnp.arange(half, dtype=F32) / half)
    ang = pos[:, None] * freq[None, :]
    cos, sin = jnp.cos(ang), jnp.sin(ang)
    cc = jnp.concatenate([cos, cos], -1)
    ss = jnp.concatenate([-sin, sin], -1)
    pad = jnp.zeros((pos.shape[0], LANE - MLA_ROPE), F32)
    return (jnp.tile(cc, (1, MLA_HEADS)), jnp.tile(ss, (1, MLA_HEADS)),
            jnp.concatenate([cc, pad], -1), jnp.concatenate([ss, pad], -1))


def _block_diag(blocks):
    h, r, c = blocks.shape
    eye = jnp.eye(h, dtype=blocks.dtype)
    return (blocks[:, :, None, :] * eye[:, None, :, None]).reshape(h * r, h * c)


def _even_weights(w_in, q_norm, w_uq, kv_norm, w_uk, w_uv, w_out):
    half = MLA_ROPE // 2
    kr0 = POOL_DIM + MLA_Q_RANK + MLA_KV_RANK
    zpad = jnp.zeros((D_MODEL, LANE - MLA_ROPE), F32)
    w_in_p = jnp.concatenate(
        [w_in, zpad, w_in[:, kr0 + half:], w_in[:, kr0:kr0 + half], zpad], -1).astype(BF16)
    uq = w_uq.reshape(MLA_Q_RANK, MLA_HEADS, MLA_NOPE + MLA_ROPE)
    nope = uq[:, :, :MLA_NOPE].reshape(MLA_Q_RANK, MLA_HEADS * MLA_NOPE)
    rope = uq[:, :, MLA_NOPE:]
    rope_a = rope.reshape(MLA_Q_RANK, MLA_HEADS * MLA_ROPE)
    rope_b = jnp.concatenate([rope[:, :, half:], rope[:, :, :half]], -1).reshape(MLA_Q_RANK, MLA_HEADS * MLA_ROPE)
    place = jnp.concatenate([jnp.eye(MLA_ROPE, dtype=F32), jnp.zeros((MLA_ROPE, LANE - MLA_ROPE), F32)], -1)
    return {
        "w_in": w_in_p,
        "q_norm": q_norm.reshape(1, -1),
        "w_uq": jnp.concatenate([nope, rope_a, rope_b], -1).astype(BF16),
        "kv_norm": kv_norm.reshape(1, -1),
        "w_bd": _block_diag(jnp.transpose(w_uk, (1, 2, 0))).astype(BF16),
        "place": _block_diag(jnp.broadcast_to(place, (MLA_HEADS,) + place.shape)).astype(BF16),
        "w_uv": _block_diag(jnp.transpose(w_uv, (1, 0, 2))).astype(BF16),
        "w_mix": w_out.astype(BF16),
    }


def _ffn_weights(g1, b1, g2, b2, w1, fb1, w2, fb2):
    r = lambda v: v.reshape(1, -1)
    return {"g1": r(g1), "b1": r(b1), "g2": r(g2), "b2": r(b2),
            "w1": w1.astype(BF16), "fb1": r(fb1), "w2": w2.astype(BF16), "fb2": r(fb2)}


def _token_tile(m):
    return TOKEN_TILE if m % TOKEN_TILE == 0 else m


def kernel(x_prompt, x_sample, cache_mla_ckv, cache_mla_krope, cache_diff_k, cache_diff_v, state_pool,
           page_table, w_in_even, pool_w, pool_scale, mla_q_norm, mla_w_uq, mla_kv_norm, mla_w_uk,
           mla_w_uv, w_out_even, w_in_odd, diff_lq1, diff_lk1, diff_lq2, diff_lk2, diff_norm, w_out_odd,
           ln1_g, ln1_b, ln2_g, ln2_b, mlp_w1, mlp_b1, mlp_w2, mlp_b2):
    bsz, s_len, d = x_prompt.shape
    db, t_len, _ = x_sample.shape
    n_pages = page_table.shape[1]
    past = n_pages * PAGE
    mp, ms = bsz * s_len, db * t_len
    tmp, tms = _token_tile(mp), _token_tile(ms)
    tq = min(FLASH_TQ, s_len)
    pps_mla = min(MLA_PAGES_PER_STEP, n_pages)
    pps_diff = min(DIFF_PAGES_PER_STEP, n_pages)
    n_pool = cache_diff_k.shape[1]
    kvw = DIFF_KVH * 2 * DIFF_HD
    cache_dk = cache_diff_k.reshape(cache_diff_k.shape[0], n_pool, DIFF_KVH * PAGE, LANE)
    cache_dv = cache_diff_v.reshape(cache_diff_v.shape[0], n_pool, DIFF_KVH * PAGE, LANE)
    cache_krt = jnp.swapaxes(cache_mla_krope, 2, 3)

    tabs_p = _rope_tables(jnp.arange(s_len, dtype=F32))
    tabs_s = tuple(jnp.tile(t, (tms // t_len, 1)) for t in _rope_tables(past + jnp.arange(t_len, dtype=F32)))
    pad_rows = lambda a, n: jnp.pad(a, ((0, 0), (0, n - a.shape[1]), (0, 0)))
    hist = 24 - t_len

    xp = x_prompt.reshape(mp, d)
    xs = x_sample.reshape(ms, d)
    outs = {k: [] for k in ("p_ckv", "p_kr", "p_dk", "p_dv", "p_pool", "s_ckv", "s_kr", "s_dk", "s_dv", "s_pool")}
    for i in range(DEPTH):
        ffn = _ffn_weights(ln1_g[i], ln1_b[i], ln2_g[i], ln2_b[i], mlp_w1[i], mlp_b1[i], mlp_w2[i], mlp_b2[i])
        if i % 2 == 0:
            e = i // 2
            wts = _even_weights(w_in_even[e], mla_q_norm[e], mla_w_uq[e], mla_kv_norm[e], mla_w_uk[e],
                                mla_w_uv[e], w_out_even[e])
            wts.update(ffn)
            pw = pool_w[e].astype(BF16)
            psc = pool_scale[e].reshape(1, -1)
            u, qcat, kcat, ckv, kr = _even_in(xp, wts, tabs_p, s_len // tmp if s_len % tmp == 0 else 1, tmp)
            u3 = u.reshape(bsz, s_len, POOL_DIM)
            pool_o = _pool(u3, pw, psc, 0, 1).reshape(mp, POOL_DIM)
            kc3 = kcat.reshape(bsz, s_len, QCAT)
            vt = jnp.transpose(kc3[:, :, :LANE].reshape(bsz, s_len // tq, tq, LANE), (0, 1, 3, 2))
            o_lat = _flash_mla(qcat.reshape(bsz, s_len, -1), kc3, vt, tq)
            xp = _post(xp, o_lat.reshape(mp, -1), pool_o, wts, True, tmp)
            outs["p_ckv"].append(ckv.reshape(bsz, s_len, -1))
            outs["p_kr"].append(kr.reshape(bsz, s_len, -1))
            outs["p_pool"].append(u3[:, s_len - POOL_STATE:])
            u, qcat, kcat, ckv, kr = _even_in(xs, wts, tabs_s, 1, tms)
            u3 = u.reshape(db, t_len, POOL_DIM)
            u_ext = jnp.concatenate([state_pool[e], u3], axis=1)
            u_pad = jnp.pad(u_ext, ((0, 0), (24 - u_ext.shape[1], 0), (0, 0)))
            bb = db
            while bb * u_pad.shape[1] > 2048 and bb % 2 == 0:
                bb //= 2
            pool_o = _pool(u_pad, pw, psc, past - hist, bb)[:, hist:].reshape(ms, POOL_DIM)
            q_rows = jnp.transpose(qcat.reshape(db, t_len, MLA_HEADS, QCAT), (0, 2, 1, 3))
            o_rows = _decode_mla(page_table, q_rows.reshape(db, MLA_HEADS * t_len, QCAT),
                                 pad_rows(kcat.reshape(db, t_len, QCAT), PAGE),
                                 cache_mla_ckv, cache_krt, e, t_len, pps_mla)
            o_lat = jnp.transpose(o_rows.reshape(db, MLA_HEADS, t_len, LANE), (0, 2, 1, 3)).reshape(ms, -1)
            xs = _post(xs, o_lat, pool_o, wts, True, tms)
            outs["s_ckv"].append(ckv.reshape(db, t_len, -1))
            outs["s_kr"].append(kr.reshape(db, t_len, -1))
            outs["s_pool"].append(u_ext[:, u_ext.shape[1] - POOL_STATE:])
        else:
            o = i // 2
            lam_init = 0.8 - 0.6 * math.exp(-0.3 * i)
            lam_vecs = tuple(v[o].reshape(1, -1) for v in (diff_lq1, diff_lk1, diff_lq2, diff_lk2))
            norm_g = diff_norm[o].reshape(1, -1)
            wts = dict(ffn)
            wts["w_mix"] = w_out_odd[o].astype(BF16)
            w_in = w_in_odd[o].astype(BF16)
            q, k, v, kb, vb = _odd_in(xp, w_in, tmp)
            vt = jnp.transpose(vb.reshape(bsz, s_len // tq, tq, DIFF_KVH, LANE), (0, 3, 1, 4, 2))
            att = _flash_diff(q.reshape(bsz, s_len, -1), kb.reshape(bsz, s_len, kvw), vt,
                              lam_vecs, norm_g.reshape(-1, 1), lam_init, tq)
            xp = _post(xp, att.reshape(mp, -1), None, wts, False, tmp)
            outs["p_dk"].append(k.reshape(bsz, s_len, DIFF_KVH, 2 * DIFF_HD))
            outs["p_dv"].append(v.reshape(bsz, s_len, DIFF_KVH, 2 * DIFF_HD))
            q, k, v, kb, vb = _odd_in(xs, w_in, tms)
            heads = DIFF_KVH * DIFF_GROUP
            q_rows = jnp.transpose(q.reshape(db, t_len, heads, LANE), (0, 2, 1, 3)).reshape(db, heads * t_len, LANE)
            new_rows = lambda a: pad_rows(a.reshape(db, t_len * DIFF_KVH, LANE), DIFF_KVH * PAGE)
            o_rows = _decode_diff(page_table, q_rows, new_rows(kb), new_rows(vb), lam_vecs, norm_g,
                                  cache_dk, cache_dv, o, t_len, lam_init, pps_diff)
            att = jnp.transpose(o_rows.reshape(db, heads, t_len, LANE), (0, 2, 1, 3)).reshape(ms, -1)
            xs = _post(xs, att, None, wts, False, tms)
            outs["s_dk"].append(k.reshape(db, t_len, DIFF_KVH, 2 * DIFF_HD))
            outs["s_dv"].append(v.reshape(db, t_len, DIFF_KVH, 2 * DIFF_HD))

    st = lambda name: jnp.stack(outs[name])
    return (xp.reshape(bsz, s_len, d), xs.reshape(db, t_len, d),
            st("p_ckv"), st("p_kr"), st("p_dk"), st("p_dv"), st("p_pool"),
            st("s_ckv"), st("s_kr"), st("s_dk"), st("s_dv"), st("s_pool"))
```

```python
import functools
import math

import jax
import jax.numpy as jnp
from jax import lax
from jax.experimental import pallas as pl
from jax.experimental.pallas import tpu as pltpu

F32 = jnp.float32
BF16 = jnp.bfloat16

D_MODEL = 1024
DEPTH = 4
PAGE = 128
POOL_DIM = 512
POOL_WINDOWS = (2, 4, 8, 16)
POOL_GD = 128
POOL_PAD = 16
POOL_STATE = 15
MLA_HEADS = 8
MLA_NOPE = 64
MLA_ROPE = 32
MLA_V = 64
MLA_Q_RANK = 256
MLA_KV_RANK = 128
MLA_SCALE = (MLA_NOPE + MLA_ROPE) ** -0.5
LOG2E = math.log2(math.e)
ROPE_THETA = 10000.0
DIFF_HEADS = 8
DIFF_HD = 64
DIFF_KVH = 2
DIFF_GROUP = 4
DIFF_SCALE = DIFF_HD ** -0.5
D_FF = 4096
ALPHA = (2 * DEPTH) ** 0.25
EPS = 1e-5
QCAT = 256
EVEN_IN_P = 1152

LANE = 128
TOKEN_TILE = 512
FF_CHUNK = 1024
MLA_PAGES_PER_STEP = 64
DIFF_PAGES_PER_STEP = 32
NEG_INF = float("-inf")


def _cparams(sem, vmem_mib):
    return pltpu.CompilerParams(dimension_semantics=sem, vmem_limit_bytes=vmem_mib * 1024 * 1024)


def _const_spec(shape):
    nd = len(shape)
    return pl.BlockSpec(shape, lambda *_: (0,) * nd, pipeline_mode=pl.Buffered(1))


def _dot(a, b):
    return jnp.dot(a, b, preferred_element_type=F32)


def _dot_nt(a, b):
    return lax.dot_general(a, b, (((1,), (1,)), ((), ())), preferred_element_type=F32)


def _rms(x, g):
    return x * lax.rsqrt(jnp.mean(x * x, -1, keepdims=True) + EPS) * g


def _ln(x, g, b):
    mu = jnp.mean(x, -1, keepdims=True)
    xc = x - mu
    var = jnp.mean(xc * xc, -1, keepdims=True)
    return xc * lax.rsqrt(var + EPS) * g + b


def _even_in_kernel(x_ref, w_in_ref, qn_ref, wuq_ref, kvn_ref, wbd_ref, place_ref,
                    cq_ref, sq_ref, ck_ref, sk_ref,
                    u_ref, qcat_ref, kcat_ref, ckv_ref, kr_ref, vt_ref):
    h = _dot(x_ref[...].astype(BF16), w_in_ref[...])
    u_ref[...] = h[:, :POOL_DIM]
    cqn = _rms(h[:, 512:768], qn_ref[...])
    q = _dot(cqn.astype(BF16), wuq_ref[...])
    rot = (q[:, 512:768] * cq_ref[...] + q[:, 768:1024] * sq_ref[...]) * (MLA_SCALE * LOG2E)
    qlat = _dot(q[:, :512].astype(BF16), wbd_ref[...]) * (MLA_SCALE * LOG2E)
    qrope = _dot(rot.astype(BF16), place_ref[...])
    for hh in range(MLA_HEADS):
        qcat_ref[:, hh * QCAT:hh * QCAT + LANE] = qlat[:, hh * LANE:(hh + 1) * LANE].astype(BF16)
        qcat_ref[:, hh * QCAT + LANE:(hh + 1) * QCAT] = qrope[:, hh * LANE:(hh + 1) * LANE].astype(BF16)
    ckvn = _rms(h[:, 768:896], kvn_ref[...])
    ckv_ref[...] = ckvn
    krr = h[:, 896:1024] * ck_ref[...] + h[:, 1024:1152] * sk_ref[...]
    kr_ref[...] = krr[:, :MLA_ROPE]
    kcat_ref[:, :LANE] = ckvn.astype(BF16)
    kcat_ref[:, LANE:] = krr.astype(BF16)
    vt_ref[0, 0] = ckvn.T.astype(BF16)


def _even_in(x2d, wts, tabs, n_tab_blocks, tm):
    m = x2d.shape[0]
    nblk = max(n_tab_blocks, 1)
    row = lambda w: pl.BlockSpec((tm, w), lambda i: (i, 0))
    tab = lambda w: pl.BlockSpec((tm, w), lambda i: (i % n_tab_blocks, 0))
    return pl.pallas_call(
        _even_in_kernel,
        grid=(m // tm,),
        in_specs=[row(D_MODEL), _const_spec((D_MODEL, EVEN_IN_P)), _const_spec((1, MLA_Q_RANK)),
                  _const_spec((MLA_Q_RANK, 1024)), _const_spec((1, MLA_KV_RANK)),
                  _const_spec((512, 1024)), _const_spec((256, 1024)),
                  tab(256), tab(256), tab(LANE), tab(LANE)],
        out_specs=[row(POOL_DIM), row(MLA_HEADS * QCAT), row(QCAT), row(MLA_KV_RANK), row(MLA_ROPE),
                   pl.BlockSpec((1, 1, LANE, tm), lambda i: (i // nblk, i % nblk, 0, 0))],
        out_shape=[jax.ShapeDtypeStruct((m, POOL_DIM), F32),
                   jax.ShapeDtypeStruct((m, MLA_HEADS * QCAT), BF16),
                   jax.ShapeDtypeStruct((m, QCAT), BF16),
                   jax.ShapeDtypeStruct((m, MLA_KV_RANK), F32),
                   jax.ShapeDtypeStruct((m, MLA_ROPE), F32),
                   jax.ShapeDtypeStruct((m // (nblk * tm), nblk, LANE, tm), BF16)],
        compiler_params=_cparams(("arbitrary",), 40),
        name="even_in",
    )(x2d, wts["w_in"], wts["q_norm"], wts["w_uq"], wts["kv_norm"], wts["w_bd"], wts["place"], *tabs)


def _pool_kernel(u_ref, w_ref, sc_ref, o_ref, ext_ref, *, start_pos):
    bb, seq, _ = u_ref.shape
    ext_ref[:, :POOL_PAD, :] = jnp.zeros((bb, POOL_PAD, POOL_DIM), F32)
    ext_ref[:, POOL_PAD:, :] = u_ref[...]
    pos = start_pos + lax.broadcasted_iota(jnp.int32, (1, seq, 1), 1).astype(F32)
    for g, win in enumerate(POOL_WINDOWS):
        lanes = slice(g * POOL_GD, (g + 1) * POOL_GD)
        tot = ext_ref[:, POOL_PAD:POOL_PAD + seq, lanes]
        for j in range(1, win):
            tot = tot + ext_ref[:, POOL_PAD - j:POOL_PAD - j + seq, lanes]
        cnt = jnp.minimum(pos + 1.0, float(win))
        d = tot / cnt - u_ref[:, :, lanes]
        y = _dot(d.reshape(bb * seq, POOL_GD).astype(BF16), w_ref[g])
        o_ref[:, :, lanes] = (y.reshape(bb, seq, POOL_GD) * sc_ref[:, lanes]).astype(BF16)


def _pool(u_ext, pool_w, pool_scale, start_pos, bb):
    nb, seq, _ = u_ext.shape
    blk = pl.BlockSpec((bb, seq, POOL_DIM), lambda i: (i, 0, 0))
    return pl.pallas_call(
        functools.partial(_pool_kernel, start_pos=float(start_pos)),
        grid=(nb // bb,),
        in_specs=[blk, _const_spec((len(POOL_WINDOWS), POOL_GD, POOL_GD)), _const_spec((1, POOL_DIM))],
        out_specs=blk,
        out_shape=jax.ShapeDtypeStruct((nb, seq, POOL_DIM), BF16),
        scratch_shapes=[pltpu.VMEM((bb, POOL_PAD + seq, POOL_DIM), F32)],
        compiler_params=_cparams(("arbitrary",), 40),
        name="pool_mix",
    )(u_ext, pool_w, pool_scale)


def _flash_sweep(qs_ref, k_ref, vt_ref, m_ref, l_ref, acc_ref, st_ref, mx_ref, *, q_idx, tq, groups):
    rows = groups * tq
    m_ref[...] = jnp.full((1, rows), NEG_INF, F32)
    l_ref[...] = jnp.zeros((1, rows), F32)
    acc_ref[...] = jnp.zeros((LANE, rows), F32)
    q = qs_ref[...]

    def scores(j, diagonal):
        start = pl.multiple_of(j * tq, tq)
        st = _dot_nt(k_ref[pl.ds(start, tq), :], q)
        if diagonal:
            key = lax.broadcasted_iota(jnp.int32, (tq, rows), 0)
            t = lax.broadcasted_iota(jnp.int32, (tq, rows), 1) % tq
            st = jnp.where(key <= t, st, NEG_INF)
        return st, jnp.max(st, 0, keepdims=True)

    def consume(j, st, mx):
        m_old = m_ref[...]
        m_new = jnp.maximum(m_old, mx)
        p = jnp.exp2(st - m_new)
        alpha = jnp.exp2(m_old - m_new)
        l_ref[...] = alpha * l_ref[...] + jnp.sum(p, 0, keepdims=True)
        acc_ref[...] = alpha * acc_ref[...] + _dot(vt_ref[j], p.astype(BF16))
        m_ref[...] = m_new

    @pl.when(q_idx == 0)
    def _():
        consume(0, *scores(0, True))

    @pl.when(q_idx > 0)
    def _():
        st_ref[...], mx_ref[...] = scores(0, False)

        def body(j, carry):
            nxt, nmx = scores(j + 1, False)
            consume(j, st_ref[...], mx_ref[...])
            st_ref[...] = nxt
            mx_ref[...] = nmx
            return carry

        lax.fori_loop(0, q_idx - 1, body, 0)
        nxt, nmx = scores(q_idx, True)
        consume(q_idx - 1, st_ref[...], mx_ref[...])
        consume(q_idx, nxt, nmx)


def _flash_scratch(rows, dk, tq):
    return [pltpu.VMEM((rows, dk), BF16), pltpu.VMEM((1, rows), F32), pltpu.VMEM((1, rows), F32),
            pltpu.VMEM((LANE, rows), F32), pltpu.VMEM((tq, rows), F32), pltpu.VMEM((1, rows), F32)]


def _flash_mla_kernel(q_ref, k_ref, vt_ref, o_ref, qs_ref, m_ref, l_ref, acc_ref, st_ref, mx_ref, *, tq):
    for hh in range(MLA_HEADS):
        qs_ref[hh * tq:(hh + 1) * tq, :] = q_ref[0, :, hh * QCAT:(hh + 1) * QCAT]
    _flash_sweep(qs_ref, k_ref.at[0], vt_ref.at[0], m_ref, l_ref, acc_ref, st_ref, mx_ref,
                 q_idx=pl.program_id(1), tq=tq, groups=MLA_HEADS)
    for hh in range(MLA_HEADS):
        cols = slice(hh * tq, (hh + 1) * tq)
        o = acc_ref[:, cols] * (1.0 / l_ref[:, cols])
        o_ref[0, :, hh * LANE:(hh + 1) * LANE] = o.T.astype(BF16)


def _flash_mla(qcat, kcat, vt, tq):
    b, s, _ = qcat.shape
    return pl.pallas_call(
        functools.partial(_flash_mla_kernel, tq=tq),
        grid=(b, s // tq),
        in_specs=[pl.BlockSpec((1, tq, MLA_HEADS * QCAT), lambda bi, i: (bi, i, 0)),
                  pl.BlockSpec((1, s, QCAT), lambda bi, i: (bi, 0, 0)),
                  pl.BlockSpec((1, s // tq, LANE, tq), lambda bi, i: (bi, 0, 0, 0))],
        out_specs=pl.BlockSpec((1, tq, MLA_HEADS * LANE), lambda bi, i: (bi, i, 0)),
        out_shape=jax.ShapeDtypeStruct((b, s, MLA_HEADS * LANE), BF16),
        scratch_shapes=_flash_scratch(MLA_HEADS * tq, QCAT, tq),
        compiler_params=_cparams(("arbitrary", "arbitrary"), 60),
        name="flash_mla",
    )(qcat, kcat, vt)


def _diff_lambda(lq1_ref, lk1_ref, lq2_ref, lk2_ref, lam_init):
    a = jnp.sum(lq1_ref[...] * lk1_ref[...], -1, keepdims=True)
    b = jnp.sum(lq2_ref[...] * lk2_ref[...], -1, keepdims=True)
    return jnp.exp(a) - jnp.exp(b) + lam_init


def _diff_mask_rows(tile):
    lane = lax.broadcasted_iota(jnp.int32, tile.shape, 1)
    zero = jnp.zeros_like(tile)
    return jnp.where(lane < DIFF_HD, tile, zero), jnp.where(lane >= DIFF_HD, tile, zero)


def _flash_diff_kernel(q_ref, k_ref, vt_ref, lq1_ref, lk1_ref, lq2_ref, lk2_ref, ngc_ref,
                       o_ref, qs_ref, m_ref, l_ref, acc_ref, st_ref, mx_ref, *, tq, lam_init):
    for g in range(DIFF_GROUP):
        qa, qb = _diff_mask_rows(q_ref[0, :, g * LANE:(g + 1) * LANE])
        qs_ref[(2 * g) * tq:(2 * g + 1) * tq, :] = qa
        qs_ref[(2 * g + 1) * tq:(2 * g + 2) * tq, :] = qb
    _flash_sweep(qs_ref, k_ref.at[0], vt_ref.at[0, 0], m_ref, l_ref, acc_ref, st_ref, mx_ref,
                 q_idx=pl.program_id(2), tq=tq, groups=2 * DIFF_GROUP)
    lam = _diff_lambda(lq1_ref, lk1_ref, lq2_ref, lk2_ref, lam_init)
    for g in range(DIFF_GROUP):
        c1 = slice((2 * g) * tq, (2 * g + 1) * tq)
        c2 = slice((2 * g + 1) * tq, (2 * g + 2) * tq)
        o = acc_ref[:, c1] * (1.0 / l_ref[:, c1]) - acc_ref[:, c2] * (lam / l_ref[:, c2])
        o = o * lax.rsqrt(jnp.mean(o * o, 0, keepdims=True) + EPS) * ngc_ref[...] * (1.0 - lam_init)
        o_ref[0, :, g * LANE:(g + 1) * LANE] = o.T.astype(BF16)


def _flash_diff(q, k, vt, lam_vecs, norm_col, lam_init, tq):
    b, s, _ = q.shape
    gw = DIFF_GROUP * LANE
    vec = _const_spec((1, DIFF_HD))
    return pl.pallas_call(
        functools.partial(_flash_diff_kernel, tq=tq, lam_init=lam_init),
        grid=(b, DIFF_KVH, s // tq),
        in_specs=[pl.BlockSpec((1, tq, gw), lambda bi, kh, i: (bi, i, kh)),
                  pl.BlockSpec((1, s, LANE), lambda bi, kh, i: (bi, 0, kh)),
                  pl.BlockSpec((1, 1, s // tq, LANE, tq), lambda bi, kh, i: (bi, kh, 0, 0, 0)),
                  vec, vec, vec, vec, _const_spec((LANE, 1))],
        out_specs=pl.BlockSpec((1, tq, gw), lambda bi, kh, i: (bi, i, kh)),
        out_shape=jax.ShapeDtypeStruct((b, s, DIFF_KVH * gw), BF16),
        scratch_shapes=_flash_scratch(2 * DIFF_GROUP * tq, LANE, tq),
        compiler_params=_cparams(("arbitrary", "arbitrary", "arbitrary"), 60),
        name="flash_diff",
    )(q, k, vt, *lam_vecs, norm_col)


def _online_update(s, vals, m_ref, l_ref, acc_ref):
    w = s.shape[1] // len(vals)
    m_old = m_ref[...]
    m_new = jnp.maximum(m_old, jnp.max(s, -1, keepdims=True))
    p = jnp.exp2(s - m_new)
    alpha = jnp.exp2(m_old - m_new)
    l_ref[...] = alpha * l_ref[...] + jnp.sum(p, -1, keepdims=True)
    pv = None
    for j, vj in enumerate(vals):
        t = _dot(p[:, j * w:(j + 1) * w].astype(BF16), vj)
        pv = t if pv is None else pv + t
    acc_ref[...] = alpha * acc_ref[...] + pv
    m_ref[...] = m_new


def _paged_copies(pt_ref, srcs, dsts, sem, b, c, slot, *, pps, n_pages, layer, start):
    base = b * n_pages + c * pps
    for j in range(pps):
        pg = pt_ref[base + j]
        for a, (src, dst) in enumerate(zip(srcs, dsts)):
            cp = pltpu.make_async_copy(src.at[layer, pg], dst(slot, j), sem.at[a, slot])
            if start:
                cp.start()
            else:
                cp.wait()


def _paged_pipeline(fetch):
    b, c = pl.program_id(0), pl.program_id(1)
    nb, nc = pl.num_programs(0), pl.num_programs(1)
    step = b * nc + c
    slot = step % 2

    @pl.when(step == 0)
    def _():
        fetch(b, c, slot, start=True)

    @pl.when(step + 1 < nb * nc)
    def _():
        wrap = c + 1 == nc
        fetch(jnp.where(wrap, b + 1, b), jnp.where(wrap, 0, c + 1), 1 - slot, start=True)

    fetch(b, c, slot, start=False)
    return slot


def _decode_mla_kernel(pt_ref, q_ref, knew_ref, ckv_hbm, krt_hbm, o_ref,
                       kbuf, rbuf, sem, m_ref, l_ref, acc_ref, *, pps, n_pages, layer, t_len):
    c = pl.program_id(1)
    dsts = [lambda s, j: kbuf.at[s, pl.ds(j * PAGE, PAGE), :],
            lambda s, j: rbuf.at[s, :, pl.ds(j * PAGE, PAGE)]]
    slot = _paged_pipeline(functools.partial(_paged_copies, pt_ref, [ckv_hbm, krt_hbm], dsts, sem,
                                             pps=pps, n_pages=n_pages, layer=layer))

    @pl.when(c == 0)
    def _():
        m_ref[...] = jnp.full(m_ref.shape, NEG_INF, F32)
        l_ref[...] = jnp.zeros(l_ref.shape, F32)
        acc_ref[...] = jnp.zeros(acc_ref.shape, F32)

    q = q_ref[0]
    kb = kbuf[slot].astype(BF16)
    s = _dot_nt(q[:, :LANE], kb) + _dot(q[:, LANE:LANE + MLA_ROPE], rbuf[slot].astype(BF16))
    _online_update(s, [kb], m_ref, l_ref, acc_ref)

    @pl.when(c == pl.num_programs(1) - 1)
    def _():
        kn = knew_ref[0]
        sn = _dot_nt(q, kn)
        t = lax.broadcasted_iota(jnp.int32, sn.shape, 0) % t_len
        j = lax.broadcasted_iota(jnp.int32, sn.shape, 1)
        _online_update(jnp.where(j <= t, sn, NEG_INF), [kn[:, :LANE]], m_ref, l_ref, acc_ref)
        o_ref[0] = (acc_ref[...] / l_ref[...]).astype(BF16)


def _decode_mla(page_table, q_rows, k_new, cache_ckv, cache_krt, layer, t_len, pps):
    db, rows, _ = q_rows.shape
    n_pages = page_table.shape[1]
    per_b = lambda shape: pl.BlockSpec(shape, lambda b, c, pt_ref: (b, 0, 0))
    hbm = pl.BlockSpec(memory_space=pl.ANY)
    grid_spec = pltpu.PrefetchScalarGridSpec(
        num_scalar_prefetch=1,
        grid=(db, n_pages // pps),
        in_specs=[per_b((1, rows, QCAT)), per_b((1, PAGE, QCAT)), hbm, hbm],
        out_specs=per_b((1, rows, LANE)),
        scratch_shapes=[pltpu.VMEM((2, pps * PAGE, MLA_KV_RANK), F32),
                        pltpu.VMEM((2, MLA_ROPE, pps * PAGE), F32),
                        pltpu.SemaphoreType.DMA((2, 2)),
                        pltpu.VMEM((rows, 1), F32), pltpu.VMEM((rows, 1), F32),
                        pltpu.VMEM((rows, LANE), F32)],
    )
    return pl.pallas_call(
        functools.partial(_decode_mla_kernel, pps=pps, n_pages=n_pages, layer=layer, t_len=t_len),
        grid_spec=grid_spec,
        out_shape=jax.ShapeDtypeStruct((db, rows, LANE), BF16),
        compiler_params=_cparams(("arbitrary", "arbitrary"), 40),
        name="decode_mla",
    )(page_table.reshape(-1), q_rows, k_new, cache_ckv, cache_krt)


def _decode_diff_kernel(pt_ref, q_ref, knew_ref, vnew_ref, lq1_ref, lk1_ref, lq2_ref, lk2_ref, ng_ref,
                        k_hbm, v_hbm, o_ref, kbuf, vbuf, sem, qs_ref, m_ref, l_ref, acc_ref,
                        *, pps, n_pages, layer, t_len, lam_init):
    c = pl.program_id(1)
    half = DIFF_GROUP * t_len
    per_kh = 2 * half
    prow = DIFF_KVH * PAGE
    dsts = [lambda s, j: kbuf.at[s, pl.ds(j * prow, prow), :],
            lambda s, j: vbuf.at[s, pl.ds(j * prow, prow), :]]
    slot = _paged_pipeline(functools.partial(_paged_copies, pt_ref, [k_hbm, v_hbm], dsts, sem,
                                             pps=pps, n_pages=n_pages, layer=layer))

    @pl.when(c == 0)
    def _():
        m_ref[...] = jnp.full(m_ref.shape, NEG_INF, F32)
        l_ref[...] = jnp.zeros(l_ref.shape, F32)
        acc_ref[...] = jnp.zeros(acc_ref.shape, F32)
        for kh in range(DIFF_KVH):
            qa, qb = _diff_mask_rows(q_ref[0, kh * half:(kh + 1) * half, :])
            qs_ref[kh * per_kh:kh * per_kh + half, :] = qa
            qs_ref[kh * per_kh + half:(kh + 1) * per_kh, :] = qb

    def masked_scores(kt, new_tokens):
        s = _dot_nt(qs_ref[...], kt)
        row = lax.broadcasted_iota(jnp.int32, s.shape, 0)
        col = lax.broadcasted_iota(jnp.int32, s.shape, 1)
        keep = (col % DIFF_KVH) == (row // per_kh)
        if new_tokens:
            keep = keep & ((col // DIFF_KVH) <= (row % t_len))
        return jnp.where(keep, s, NEG_INF)

    def update(kt, vt, new_tokens):
        _online_update(masked_scores(kt, new_tokens), [vt], m_ref, l_ref, acc_ref)

    n = pps * prow
    halves = [slice(0, n // 2), slice(n // 2, n)]
    ss = [masked_scores(kbuf[slot, h, :].astype(BF16), False) for h in halves]
    for h, sh in zip(halves, ss):
        _online_update(sh, [vbuf[slot, h, :].astype(BF16)], m_ref, l_ref, acc_ref)

    @pl.when(c == pl.num_programs(1) - 1)
    def _():
        update(knew_ref[0], vnew_ref[0], True)
        lam = _diff_lambda(lq1_ref, lk1_ref, lq2_ref, lk2_ref, lam_init)
        for kh in range(DIFF_KVH):
            r1 = slice(kh * per_kh, kh * per_kh + half)
            r2 = slice(kh * per_kh + half, (kh + 1) * per_kh)
            o = acc_ref[r1, :] / l_ref[r1, :] - lam * (acc_ref[r2, :] / l_ref[r2, :])
            o = _rms(o, ng_ref[...]) * (1.0 - lam_init)
            o_ref[0, kh * half:(kh + 1) * half, :] = o.astype(BF16)


def _decode_diff(page_table, q_rows, k_new, v_new, lam_vecs, norm_g, cache_k, cache_v,
                 layer, t_len, lam_init, pps):
    db, rows, _ = q_rows.shape
    n_pages = page_table.shape[1]
    prow = DIFF_KVH * PAGE
    const = lambda shape: pl.BlockSpec(shape, lambda b, c, pt_ref: (0,) * len(shape))
    per_b = lambda shape: pl.BlockSpec(shape, lambda b, c, pt_ref: (b, 0, 0))
    hbm = pl.BlockSpec(memory_space=pl.ANY)
    grid_spec = pltpu.PrefetchScalarGridSpec(
        num_scalar_prefetch=1,
        grid=(db, n_pages // pps),
        in_specs=[per_b((1, rows, LANE)), per_b((1, prow, LANE)), per_b((1, prow, LANE)),
                  const((1, DIFF_HD)), const((1, DIFF_HD)), const((1, DIFF_HD)), const((1, DIFF_HD)),
                  const((1, LANE)), hbm, hbm],
        out_specs=per_b((1, rows, LANE)),
        scratch_shapes=[pltpu.VMEM((2, pps * prow, LANE), F32), pltpu.VMEM((2, pps * prow, LANE), F32),
                        pltpu.SemaphoreType.DMA((2, 2)),
                        pltpu.VMEM((2 * rows, LANE), BF16), pltpu.VMEM((2 * rows, 1), F32),
                        pltpu.VMEM((2 * rows, 1), F32), pltpu.VMEM((2 * rows, LANE), F32)],
    )
    return pl.pallas_call(
        functools.partial(_decode_diff_kernel, pps=pps, n_pages=n_pages, layer=layer, t_len=t_len,
                          lam_init=lam_init),
        grid_spec=grid_spec,
        out_shape=jax.ShapeDtypeStruct((db, rows, LANE), BF16),
        compiler_params=_cparams(("arbitrary", "arbitrary"), 56),
        name="decode_diff",
    )(page_table.reshape(-1), q_rows, k_new, v_new, *lam_vecs, norm_g, cache_k, cache_v)


def _odd_in_kernel(x_ref, w_ref, q_ref, k_ref, v_ref, kb_ref, vb_ref, vt_ref):
    h = _dot(x_ref[...].astype(BF16), w_ref[...])
    nq = DIFF_HEADS * 2 * DIFF_HD
    nkv = DIFF_KVH * 2 * DIFF_HD
    q_ref[...] = (h[:, :nq] * (DIFF_SCALE * LOG2E)).astype(BF16)
    k = h[:, nq:nq + nkv]
    v = h[:, nq + nkv:]
    tm = k.shape[0]
    for kh in range(DIFF_KVH):
        k_ref[pl.ds(kh, tm, stride=DIFF_KVH), :] = k[:, kh * LANE:(kh + 1) * LANE]
        v_ref[pl.ds(kh, tm, stride=DIFF_KVH), :] = v[:, kh * LANE:(kh + 1) * LANE]
        vt_ref[0, kh, 0] = v[:, kh * LANE:(kh + 1) * LANE].T.astype(BF16)
    kb_ref[...] = k.astype(BF16)
    vb_ref[...] = v.astype(BF16)


def _odd_in(x2d, w_in, tm, nblk):
    m = x2d.shape[0]
    nq = DIFF_HEADS * 2 * DIFF_HD
    nkv = DIFF_KVH * 2 * DIFF_HD
    row = lambda w: pl.BlockSpec((tm, w), lambda i: (i, 0))
    return pl.pallas_call(
        _odd_in_kernel,
        grid=(m // tm,),
        in_specs=[row(D_MODEL), _const_spec((D_MODEL, nq + 2 * nkv))],
        out_specs=[row(nq), pl.BlockSpec((DIFF_KVH * tm, LANE), lambda i: (i, 0)),
                   pl.BlockSpec((DIFF_KVH * tm, LANE), lambda i: (i, 0)), row(nkv), row(nkv),
                   pl.BlockSpec((1, DIFF_KVH, 1, LANE, tm), lambda i: (i // nblk, 0, i % nblk, 0, 0))],
        out_shape=[jax.ShapeDtypeStruct((m, nq), BF16),
                   jax.ShapeDtypeStruct((DIFF_KVH * m, LANE), F32), jax.ShapeDtypeStruct((DIFF_KVH * m, LANE), F32),
                   jax.ShapeDtypeStruct((m, nkv), BF16), jax.ShapeDtypeStruct((m, nkv), BF16),
                   jax.ShapeDtypeStruct((m // (nblk * tm), DIFF_KVH, nblk, LANE, tm), BF16)],
        compiler_params=_cparams(("arbitrary",), 40),
        name="odd_in",
    )(x2d, w_in)


def _post_kernel(*refs, even):
    if even:
        (x_ref, a_ref, pool_ref, wuv_ref, wmix_ref, g1_ref, b1_ref, g2_ref, b2_ref,
         w1_ref, fb1_ref, w2_ref, fb2_ref, o_ref) = refs
        o = _dot(a_ref[...], wuv_ref[...]).astype(BF16)
        mix = _dot(pool_ref[...], wmix_ref[:POOL_DIM, :]) + _dot(o, wmix_ref[POOL_DIM:, :])
    else:
        (x_ref, a_ref, wmix_ref, g1_ref, b1_ref, g2_ref, b2_ref,
         w1_ref, fb1_ref, w2_ref, fb2_ref, o_ref) = refs
        mix = _dot(a_ref[...], wmix_ref[...])
    x1 = _ln(ALPHA * x_ref[...] + mix, g1_ref[...], b1_ref[...])
    x1b = x1.astype(BF16)
    ffn = None
    for c in range(D_FF // FF_CHUNK):
        cols = slice(c * FF_CHUNK, (c + 1) * FF_CHUNK)
        hdn = jnp.square(jnp.maximum(_dot(x1b, w1_ref[:, cols]) + fb1_ref[:, cols], 0.0))
        part = _dot(hdn.astype(BF16), w2_ref[cols, :])
        ffn = part if ffn is None else ffn + part
    o_ref[...] = _ln(ALPHA * x1 + (ffn + fb2_ref[...]), g2_ref[...], b2_ref[...])


def _post(x2d, a, extra, wts, even, tm):
    m = x2d.shape[0]
    row = lambda w: pl.BlockSpec((tm, w), lambda i: (i, 0))
    vec = _const_spec((1, D_MODEL))
    ins = [x2d, a]
    specs = [row(D_MODEL), row(a.shape[1])]
    if even:
        ins += [extra, wts["w_uv"]]
        specs += [row(POOL_DIM), _const_spec((MLA_HEADS * LANE, MLA_HEADS * MLA_V))]
    ins += [wts["w_mix"], wts["g1"], wts["b1"], wts["g2"], wts["b2"],
            wts["w1"], wts["fb1"], wts["w2"], wts["fb2"]]
    specs += [_const_spec((D_MODEL, D_MODEL)), vec, vec, vec, vec,
              _const_spec((D_MODEL, D_FF)), _const_spec((1, D_FF)), _const_spec((D_FF, D_MODEL)), vec]
    return pl.pallas_call(
        functools.partial(_post_kernel, even=even),
        grid=(m // tm,),
        in_specs=specs,
        out_specs=row(D_MODEL),
        out_shape=jax.ShapeDtypeStruct((m, D_MODEL), F32),
        compiler_params=_cparams(("arbitrary",), 56),
        name="post_even" if even else "post_odd",
    )(*ins)


def _rope_tables(pos):
    half = MLA_ROPE // 2
    freq = ROPE_THETA ** (-jnp.arange(half, dtype=F32) / half)
    ang = pos[:, None] * freq[None, :]
    cos, sin = jnp.cos(ang), jnp.sin(ang)
    cc = jnp.concatenate([cos, cos], -1)
    ss = jnp.concatenate([-sin, sin], -1)
    pad = jnp.zeros((pos.shape[0], LANE - MLA_ROPE), F32)
    return (jnp.tile(cc, (1, MLA_HEADS)), jnp.tile(ss, (1, MLA_HEADS)),
            jnp.concatenate([cc, pad], -1), jnp.concatenate([ss, pad], -1))


def _block_diag(blocks):
    h, r, c = blocks.shape
    eye = jnp.eye(h, dtype=blocks.dtype)
    return (blocks[:, :, None, :] * eye[:, None, :, None]).reshape(h * r, h * c)


def _even_weights(w_in, q_norm, w_uq, kv_norm, w_uk, w_uv, w_out):
    half = MLA_ROPE // 2
    kr0 = POOL_DIM + MLA_Q_RANK + MLA_KV_RANK
    zpad = jnp.zeros((D_MODEL, LANE - MLA_ROPE), F32)
    w_in_p = jnp.concatenate(
        [w_in, zpad, w_in[:, kr0 + half:], w_in[:, kr0:kr0 + half], zpad], -1).astype(BF16)
    uq = w_uq.reshape(MLA_Q_RANK, MLA_HEADS, MLA_NOPE + MLA_ROPE)
    nope = uq[:, :, :MLA_NOPE].reshape(MLA_Q_RANK, MLA_HEADS * MLA_NOPE)
    rope = uq[:, :, MLA_NOPE:]
    rope_a = rope.reshape(MLA_Q_RANK, MLA_HEADS * MLA_ROPE)
    rope_b = jnp.concatenate([rope[:, :, half:], rope[:, :, :half]], -1).reshape(MLA_Q_RANK, MLA_HEADS * MLA_ROPE)
    place = jnp.concatenate([jnp.eye(MLA_ROPE, dtype=F32), jnp.zeros((MLA_ROPE, LANE - MLA_ROPE), F32)], -1)
    return {
        "w_in": w_in_p,
        "q_norm": q_norm.reshape(1, -1),
        "w_uq": jnp.concatenate([nope, rope_a, rope_b], -1).astype(BF16),
        "kv_norm": kv_norm.reshape(1, -1),
        "w_bd": _block_diag(jnp.transpose(w_uk, (1, 2, 0))).astype(BF16),
        "place": _block_diag(jnp.broadcast_to(place, (MLA_HEADS,) + place.shape)).astype(BF16),
        "w_uv": _block_diag(jnp.transpose(w_uv, (1, 0, 2))).astype(BF16),
        "w_mix": w_out.astype(BF16),
    }


def _ffn_weights(g1, b1, g2, b2, w1, fb1, w2, fb2):
    r = lambda v: v.reshape(1, -1)
    return {"g1": r(g1), "b1": r(b1), "g2": r(g2), "b2": r(b2),
            "w1": w1.astype(BF16), "fb1": r(fb1), "w2": w2.astype(BF16), "fb2": r(fb2)}


def _token_tile(m):
    return TOKEN_TILE if m % TOKEN_TILE == 0 else m


def kernel(x_prompt, x_sample, cache_mla_ckv, cache_mla_krope, cache_diff_k, cache_diff_v, state_pool,
           page_table, w_in_even, pool_w, pool_scale, mla_q_norm, mla_w_uq, mla_kv_norm, mla_w_uk,
           mla_w_uv, w_out_even, w_in_odd, diff_lq1, diff_lk1, diff_lq2, diff_lk2, diff_norm, w_out_odd,
           ln1_g, ln1_b, ln2_g, ln2_b, mlp_w1, mlp_b1, mlp_w2, mlp_b2):
    bsz, s_len, d = x_prompt.shape
    db, t_len, _ = x_sample.shape
    n_pages = page_table.shape[1]
    past = n_pages * PAGE
    mp, ms = bsz * s_len, db * t_len
    tmp, tms = _token_tile(mp), _token_tile(ms)
    tq = tmp
    pps_mla = min(MLA_PAGES_PER_STEP, n_pages)
    pps_diff = min(DIFF_PAGES_PER_STEP, n_pages)
    n_pool = cache_diff_k.shape[1]
    kvw = DIFF_KVH * 2 * DIFF_HD
    cache_dk = cache_diff_k.reshape(cache_diff_k.shape[0], n_pool, DIFF_KVH * PAGE, LANE)
    cache_dv = cache_diff_v.reshape(cache_diff_v.shape[0], n_pool, DIFF_KVH * PAGE, LANE)
    cache_krt = jnp.swapaxes(cache_mla_krope, 2, 3)

    tabs_p = _rope_tables(jnp.arange(s_len, dtype=F32))
    tabs_s = tuple(jnp.tile(t, (tms // t_len, 1)) for t in _rope_tables(past + jnp.arange(t_len, dtype=F32)))
    pad_rows = lambda a, n: jnp.pad(a, ((0, 0), (0, n - a.shape[1]), (0, 0)))
    hist = 24 - t_len

    xp = x_prompt.reshape(mp, d)
    xs = x_sample.reshape(ms, d)
    outs = {k: [] for k in ("p_ckv", "p_kr", "p_dk", "p_dv", "p_pool", "s_ckv", "s_kr", "s_dk", "s_dv", "s_pool")}
    for i in range(DEPTH):
        ffn = _ffn_weights(ln1_g[i], ln1_b[i], ln2_g[i], ln2_b[i], mlp_w1[i], mlp_b1[i], mlp_w2[i], mlp_b2[i])
        if i % 2 == 0:
            e = i // 2
            wts = _even_weights(w_in_even[e], mla_q_norm[e], mla_w_uq[e], mla_kv_norm[e], mla_w_uk[e],
                                mla_w_uv[e], w_out_even[e])
            wts.update(ffn)
            pw = pool_w[e].astype(BF16)
            psc = pool_scale[e].reshape(1, -1)
            u, qcat, kcat, ckv, kr, vt = _even_in(xp, wts, tabs_p, s_len // tmp, tmp)
            u3 = u.reshape(bsz, s_len, POOL_DIM)
            pool_o = _pool(u3, pw, psc, 0, 1).reshape(mp, POOL_DIM)
            o_lat = _flash_mla(qcat.reshape(bsz, s_len, -1), kcat.reshape(bsz, s_len, QCAT), vt, tq)
            xp = _post(xp, o_lat.reshape(mp, -1), pool_o, wts, True, tmp)
            outs["p_ckv"].append(ckv.reshape(bsz, s_len, -1))
            outs["p_kr"].append(kr.reshape(bsz, s_len, -1))
            outs["p_pool"].append(u3[:, s_len - POOL_STATE:])
            u, qcat, kcat, ckv, kr, _ = _even_in(xs, wts, tabs_s, 1, tms)
            u3 = u.reshape(db, t_len, POOL_DIM)
            u_ext = jnp.concatenate([state_pool[e], u3], axis=1)
            u_pad = jnp.pad(u_ext, ((0, 0), (24 - u_ext.shape[1], 0), (0, 0)))
            bb = db
            while bb * u_pad.shape[1] > 2048 and bb % 2 == 0:
                bb //= 2
            pool_o = _pool(u_pad, pw, psc, past - hist, bb)[:, hist:].reshape(ms, POOL_DIM)
            q_rows = jnp.transpose(qcat.reshape(db, t_len, MLA_HEADS, QCAT), (0, 2, 1, 3))
            o_rows = _decode_mla(page_table, q_rows.reshape(db, MLA_HEADS * t_len, QCAT),
                                 pad_rows(kcat.reshape(db, t_len, QCAT), PAGE),
                                 cache_mla_ckv, cache_krt, e, t_len, pps_mla)
            o_lat = jnp.transpose(o_rows.reshape(db, MLA_HEADS, t_len, LANE), (0, 2, 1, 3)).reshape(ms, -1)
            xs = _post(xs, o_lat, pool_o, wts, True, tms)
            outs["s_ckv"].append(ckv.reshape(db, t_len, -1))
            outs["s_kr"].append(kr.reshape(db, t_len, -1))
            outs["s_pool"].append(u_ext[:, u_ext.shape[1] - POOL_STATE:])
        else:
            o = i // 2
            lam_init = 0.8 - 0.6 * math.exp(-0.3 * i)
            lam_vecs = tuple(v[o].reshape(1, -1) for v in (diff_lq1, diff_lk1, diff_lq2, diff_lk2))
            norm_g = diff_norm[o].reshape(1, -1)
            wts = dict(ffn)
            wts["w_mix"] = w_out_odd[o].astype(BF16)
            w_in = w_in_odd[o].astype(BF16)
            q, k, v, kb, _, vt = _odd_in(xp, w_in, tmp, s_len // tmp)
            att = _flash_diff(q.reshape(bsz, s_len, -1), kb.reshape(bsz, s_len, kvw), vt,
                              lam_vecs, norm_g.reshape(-1, 1), lam_init, tq)
            xp = _post(xp, att.reshape(mp, -1), None, wts, False, tmp)
            outs["p_dk"].append(k.reshape(bsz, s_len, DIFF_KVH, 2 * DIFF_HD))
            outs["p_dv"].append(v.reshape(bsz, s_len, DIFF_KVH, 2 * DIFF_HD))
            q, k, v, kb, vb, _ = _odd_in(xs, w_in, tms, 1)
            heads = DIFF_KVH * DIFF_GROUP
            q_rows = jnp.transpose(q.reshape(db, t_len, heads, LANE), (0, 2, 1, 3)).reshape(db, heads * t_len, LANE)
            new_rows = lambda a: pad_rows(a.reshape(db, t_len * DIFF_KVH, LANE), DIFF_KVH * PAGE)
            o_rows = _decode_diff(page_table, q_rows, new_rows(kb), new_rows(vb), lam_vecs, norm_g,
                                  cache_dk, cache_dv, o, t_len, lam_init, pps_diff)
            att = jnp.transpose(o_rows.reshape(db, heads, t_len, LANE), (0, 2, 1, 3)).reshape(ms, -1)
            xs = _post(xs, att, None, wts, False, tms)
            outs["s_dk"].append(k.reshape(db, t_len, DIFF_KVH, 2 * DIFF_HD))
            outs["s_dv"].append(v.reshape(db, t_len, DIFF_KVH, 2 * DIFF_HD))

    st = lambda name: jnp.stack(outs[name])
    return (xp.reshape(bsz, s_len, d), xs.reshape(db, t_len, d),
            st("p_ckv"), st("p_kr"), st("p_dk"), st("p_dv"), st("p_pool"),
            st("s_ckv"), st("s_kr"), st("s_dk"), st("s_dv"), st("s_pool"))
```

```python
import functools
import math

import jax
import jax.numpy as jnp
from jax import lax
from jax.experimental import pallas as pl
from jax.experimental.pallas import tpu as pltpu

F32 = jnp.float32
BF16 = jnp.bfloat16

D_MODEL = 1024
DEPTH = 4
PAGE = 128
POOL_DIM = 512
POOL_WINDOWS = (2, 4, 8, 16)
POOL_GD = 128
POOL_PAD = 16
POOL_STATE = 15
MLA_HEADS = 8
MLA_NOPE = 64
MLA_ROPE = 32
MLA_V = 64
MLA_Q_RANK = 256
MLA_KV_RANK = 128
MLA_SCALE = (MLA_NOPE + MLA_ROPE) ** -0.5
LOG2E = math.log2(math.e)
ROPE_THETA = 10000.0
DIFF_HEADS = 8
DIFF_HD = 64
DIFF_KVH = 2
DIFF_GROUP = 4
DIFF_SCALE = DIFF_HD ** -0.5
D_FF = 4096
ALPHA = (2 * DEPTH) ** 0.25
EPS = 1e-5
QCAT = 256
EVEN_IN_P = 1152

LANE = 128
TOKEN_TILE = 512
FF_CHUNK = 1024
MLA_PAGES_PER_STEP = 64
DIFF_PAGES_PER_STEP = 32
NEG_INF = float("-inf")


def _cparams(sem, vmem_mib):
    return pltpu.CompilerParams(dimension_semantics=sem, vmem_limit_bytes=vmem_mib * 1024 * 1024)


def _const_spec(shape):
    nd = len(shape)
    return pl.BlockSpec(shape, lambda *_: (0,) * nd, pipeline_mode=pl.Buffered(1))


def _dot(a, b):
    return jnp.dot(a, b, preferred_element_type=F32)


def _dot_nt(a, b):
    return lax.dot_general(a, b, (((1,), (1,)), ((), ())), preferred_element_type=F32)


def _rms(x, g):
    return x * lax.rsqrt(jnp.mean(x * x, -1, keepdims=True) + EPS) * g


def _ln(x, g, b):
    mu = jnp.mean(x, -1, keepdims=True)
    xc = x - mu
    var = jnp.mean(xc * xc, -1, keepdims=True)
    return xc * lax.rsqrt(var + EPS) * g + b


def _even_in_kernel(x_ref, w_in_ref, qn_ref, wuq_ref, kvn_ref, wbd_ref, place_ref,
                    cq_ref, sq_ref, ck_ref, sk_ref,
                    u_ref, qcat_ref, kcat_ref, ckv_ref, kr_ref, vt_ref):
    h = _dot(x_ref[...].astype(BF16), w_in_ref[...])
    u_ref[...] = h[:, :POOL_DIM]
    cqn = _rms(h[:, 512:768], qn_ref[...])
    q = _dot(cqn.astype(BF16), wuq_ref[...])
    rot = (q[:, 512:768] * cq_ref[...] + q[:, 768:1024] * sq_ref[...]) * (MLA_SCALE * LOG2E)
    qlat = _dot(q[:, :512].astype(BF16), wbd_ref[...]) * (MLA_SCALE * LOG2E)
    qrope = _dot(rot.astype(BF16), place_ref[...])
    for hh in range(MLA_HEADS):
        qcat_ref[:, hh * QCAT:hh * QCAT + LANE] = qlat[:, hh * LANE:(hh + 1) * LANE].astype(BF16)
        qcat_ref[:, hh * QCAT + LANE:(hh + 1) * QCAT] = qrope[:, hh * LANE:(hh + 1) * LANE].astype(BF16)
    ckvn = _rms(h[:, 768:896], kvn_ref[...])
    ckv_ref[...] = ckvn
    krr = h[:, 896:1024] * ck_ref[...] + h[:, 1024:1152] * sk_ref[...]
    kr_ref[...] = krr[:, :MLA_ROPE]
    kcat_ref[:, :LANE] = ckvn.astype(BF16)
    kcat_ref[:, LANE:] = krr.astype(BF16)
    vt_ref[0, 0] = ckvn.T.astype(BF16)


def _even_in(x2d, wts, tabs, n_tab_blocks, tm):
    m = x2d.shape[0]
    nblk = max(n_tab_blocks, 1)
    row = lambda w: pl.BlockSpec((tm, w), lambda i: (i, 0))
    tab = lambda w: pl.BlockSpec((tm, w), lambda i: (i % n_tab_blocks, 0))
    return pl.pallas_call(
        _even_in_kernel,
        grid=(m // tm,),
        in_specs=[row(D_MODEL), _const_spec((D_MODEL, EVEN_IN_P)), _const_spec((1, MLA_Q_RANK)),
                  _const_spec((MLA_Q_RANK, 1024)), _const_spec((1, MLA_KV_RANK)),
                  _const_spec((512, 1024)), _const_spec((256, 1024)),
                  tab(256), tab(256), tab(LANE), tab(LANE)],
        out_specs=[row(POOL_DIM), row(MLA_HEADS * QCAT), row(QCAT), row(MLA_KV_RANK), row(MLA_ROPE),
                   pl.BlockSpec((1, 1, LANE, tm), lambda i: (i // nblk, i % nblk, 0, 0))],
        out_shape=[jax.ShapeDtypeStruct((m, POOL_DIM), F32),
                   jax.ShapeDtypeStruct((m, MLA_HEADS * QCAT), BF16),
                   jax.ShapeDtypeStruct((m, QCAT), BF16),
                   jax.ShapeDtypeStruct((m, MLA_KV_RANK), F32),
                   jax.ShapeDtypeStruct((m, MLA_ROPE), F32),
                   jax.ShapeDtypeStruct((m // (nblk * tm), nblk, LANE, tm), BF16)],
        compiler_params=_cparams(("arbitrary",), 40),
        name="even_in",
    )(x2d, wts["w_in"], wts["q_norm"], wts["w_uq"], wts["kv_norm"], wts["w_bd"], wts["place"], *tabs)


def _pool_kernel(u_ref, w_ref, sc_ref, o_ref, ext_ref, *, start_pos):
    bb, seq, _ = u_ref.shape
    ext_ref[:, :POOL_PAD, :] = jnp.zeros((bb, POOL_PAD, POOL_DIM), F32)
    ext_ref[:, POOL_PAD:, :] = u_ref[...]
    pos = start_pos + lax.broadcasted_iota(jnp.int32, (1, seq, 1), 1).astype(F32)
    for g, win in enumerate(POOL_WINDOWS):
        lanes = slice(g * POOL_GD, (g + 1) * POOL_GD)
        tot = ext_ref[:, POOL_PAD:POOL_PAD + seq, lanes]
        for j in range(1, win):
            tot = tot + ext_ref[:, POOL_PAD - j:POOL_PAD - j + seq, lanes]
        cnt = jnp.minimum(pos + 1.0, float(win))
        d = tot / cnt - u_ref[:, :, lanes]
        y = _dot(d.reshape(bb * seq, POOL_GD).astype(BF16), w_ref[g])
        o_ref[:, :, lanes] = (y.reshape(bb, seq, POOL_GD) * sc_ref[:, lanes]).astype(BF16)


def _pool(u_ext, pool_w, pool_scale, start_pos, bb):
    nb, seq, _ = u_ext.shape
    blk = pl.BlockSpec((bb, seq, POOL_DIM), lambda i: (i, 0, 0))
    return pl.pallas_call(
        functools.partial(_pool_kernel, start_pos=float(start_pos)),
        grid=(nb // bb,),
        in_specs=[blk, _const_spec((len(POOL_WINDOWS), POOL_GD, POOL_GD)), _const_spec((1, POOL_DIM))],
        out_specs=blk,
        out_shape=jax.ShapeDtypeStruct((nb, seq, POOL_DIM), BF16),
        scratch_shapes=[pltpu.VMEM((bb, POOL_PAD + seq, POOL_DIM), F32)],
        compiler_params=_cparams(("arbitrary",), 40),
        name="pool_mix",
    )(u_ext, pool_w, pool_scale)


def _flash_sweep(qs_ref, k_ref, vt_ref, m_ref, l_ref, acc_ref, st_ref, mx_ref, *, q_idx, tq, groups):
    rows = groups * tq
    m_ref[...] = jnp.full((1, rows), NEG_INF, F32)
    l_ref[...] = jnp.zeros((1, rows), F32)
    acc_ref[...] = jnp.zeros((LANE, rows), F32)
    q = qs_ref[...]

    def scores(j, diagonal):
        start = pl.multiple_of(j * tq, tq)
        st = _dot_nt(k_ref[pl.ds(start, tq), :], q)
        if diagonal:
            key = lax.broadcasted_iota(jnp.int32, (tq, rows), 0)
            t = lax.broadcasted_iota(jnp.int32, (tq, rows), 1) % tq
            st = jnp.where(key <= t, st, NEG_INF)
        return st, jnp.max(st, 0, keepdims=True)

    def consume(j, st, mx):
        m_old = m_ref[...]
        m_new = jnp.maximum(m_old, mx)
        p = jnp.exp2(st - m_new)
        alpha = jnp.exp2(m_old - m_new)
        l_ref[...] = alpha * l_ref[...] + jnp.sum(p, 0, keepdims=True)
        acc_ref[...] = alpha * acc_ref[...] + _dot(vt_ref[j], p.astype(BF16))
        m_ref[...] = m_new

    @pl.when(q_idx == 0)
    def _():
        consume(0, *scores(0, True))

    @pl.when(q_idx > 0)
    def _():
        st_ref[...], mx_ref[...] = scores(0, False)

        def body(j, carry):
            nxt, nmx = scores(j + 1, False)
            consume(j, st_ref[...], mx_ref[...])
            st_ref[...] = nxt
            mx_ref[...] = nmx
            return carry

        lax.fori_loop(0, q_idx - 1, body, 0)
        nxt, nmx = scores(q_idx, True)
        consume(q_idx - 1, st_ref[...], mx_ref[...])
        consume(q_idx, nxt, nmx)


def _flash_scratch(rows, dk, tq):
    return [pltpu.VMEM((rows, dk), BF16), pltpu.VMEM((1, rows), F32), pltpu.VMEM((1, rows), F32),
            pltpu.VMEM((LANE, rows), F32), pltpu.VMEM((tq, rows), F32), pltpu.VMEM((1, rows), F32)]


def _flash_mla_kernel(q_ref, k_ref, vt_ref, o_ref, qs_ref, m_ref, l_ref, acc_ref, st_ref, mx_ref, *, tq):
    for hh in range(MLA_HEADS):
        qs_ref[hh * tq:(hh + 1) * tq, :] = q_ref[0, :, hh * QCAT:(hh + 1) * QCAT]
    _flash_sweep(qs_ref, k_ref.at[0], vt_ref.at[0], m_ref, l_ref, acc_ref, st_ref, mx_ref,
                 q_idx=pl.program_id(1), tq=tq, groups=MLA_HEADS)
    for hh in range(MLA_HEADS):
        cols = slice(hh * tq, (hh + 1) * tq)
        o = acc_ref[:, cols] * (1.0 / l_ref[:, cols])
        o_ref[0, :, hh * LANE:(hh + 1) * LANE] = o.T.astype(BF16)


def _flash_mla(qcat, kcat, vt, tq):
    b, s, _ = qcat.shape
    return pl.pallas_call(
        functools.partial(_flash_mla_kernel, tq=tq),
        grid=(b, s // tq),
        in_specs=[pl.BlockSpec((1, tq, MLA_HEADS * QCAT), lambda bi, i: (bi, i, 0)),
                  pl.BlockSpec((1, s, QCAT), lambda bi, i: (bi, 0, 0)),
                  pl.BlockSpec((1, s // tq, LANE, tq), lambda bi, i: (bi, 0, 0, 0))],
        out_specs=pl.BlockSpec((1, tq, MLA_HEADS * LANE), lambda bi, i: (bi, i, 0)),
        out_shape=jax.ShapeDtypeStruct((b, s, MLA_HEADS * LANE), BF16),
        scratch_shapes=_flash_scratch(MLA_HEADS * tq, QCAT, tq),
        compiler_params=_cparams(("arbitrary", "arbitrary"), 60),
        name="flash_mla",
    )(qcat, kcat, vt)


def _diff_lambda(lq1_ref, lk1_ref, lq2_ref, lk2_ref, lam_init):
    a = jnp.sum(lq1_ref[...] * lk1_ref[...], -1, keepdims=True)
    b = jnp.sum(lq2_ref[...] * lk2_ref[...], -1, keepdims=True)
    return jnp.exp(a) - jnp.exp(b) + lam_init


def _diff_mask_rows(tile):
    lane = lax.broadcasted_iota(jnp.int32, tile.shape, 1)
    zero = jnp.zeros_like(tile)
    return jnp.where(lane < DIFF_HD, tile, zero), jnp.where(lane >= DIFF_HD, tile, zero)


def _flash_diff_kernel(q_ref, k_ref, vt_ref, lq1_ref, lk1_ref, lq2_ref, lk2_ref, ngc_ref,
                       o_ref, qs_ref, m_ref, l_ref, acc_ref, st_ref, mx_ref, *, tq, lam_init):
    for g in range(DIFF_GROUP):
        qa, qb = _diff_mask_rows(q_ref[0, :, g * LANE:(g + 1) * LANE])
        qs_ref[(2 * g) * tq:(2 * g + 1) * tq, :] = qa
        qs_ref[(2 * g + 1) * tq:(2 * g + 2) * tq, :] = qb
    _flash_sweep(qs_ref, k_ref.at[0], vt_ref.at[0, 0], m_ref, l_ref, acc_ref, st_ref, mx_ref,
                 q_idx=pl.program_id(2), tq=tq, groups=2 * DIFF_GROUP)
    lam = _diff_lambda(lq1_ref, lk1_ref, lq2_ref, lk2_ref, lam_init)
    for g in range(DIFF_GROUP):
        c1 = slice((2 * g) * tq, (2 * g + 1) * tq)
        c2 = slice((2 * g + 1) * tq, (2 * g + 2) * tq)
        o = acc_ref[:, c1] * (1.0 / l_ref[:, c1]) - acc_ref[:, c2] * (lam / l_ref[:, c2])
        o = o * lax.rsqrt(jnp.mean(o * o, 0, keepdims=True) + EPS) * ngc_ref[...] * (1.0 - lam_init)
        o_ref[0, :, g * LANE:(g + 1) * LANE] = o.T.astype(BF16)


def _flash_diff(q, k, vt, lam_vecs, norm_col, lam_init, tq):
    b, s, _ = q.shape
    gw = DIFF_GROUP * LANE
    vec = _const_spec((1, DIFF_HD))
    return pl.pallas_call(
        functools.partial(_flash_diff_kernel, tq=tq, lam_init=lam_init),
        grid=(b, DIFF_KVH, s // tq),
        in_specs=[pl.BlockSpec((1, tq, gw), lambda bi, kh, i: (bi, i, kh)),
                  pl.BlockSpec((1, s, LANE), lambda bi, kh, i: (bi, 0, kh)),
                  pl.BlockSpec((1, 1, s // tq, LANE, tq), lambda bi, kh, i: (bi, kh, 0, 0, 0)),
                  vec, vec, vec, vec, _const_spec((LANE, 1))],
        out_specs=pl.BlockSpec((1, tq, gw), lambda bi, kh, i: (bi, i, kh)),
        out_shape=jax.ShapeDtypeStruct((b, s, DIFF_KVH * gw), BF16),
        scratch_shapes=_flash_scratch(2 * DIFF_GROUP * tq, LANE, tq),
        compiler_params=_cparams(("arbitrary", "arbitrary", "arbitrary"), 60),
        name="flash_diff",
    )(q, k, vt, *lam_vecs, norm_col)


def _online_update(s, vals, m_ref, l_ref, acc_ref):
    w = s.shape[1] // len(vals)
    m_old = m_ref[...]
    m_new = jnp.maximum(m_old, jnp.max(s, -1, keepdims=True))
    p = jnp.exp2(s - m_new)
    alpha = jnp.exp2(m_old - m_new)
    l_ref[...] = alpha * l_ref[...] + jnp.sum(p, -1, keepdims=True)
    pv = None
    for j, vj in enumerate(vals):
        t = _dot(p[:, j * w:(j + 1) * w].astype(BF16), vj)
        pv = t if pv is None else pv + t
    acc_ref[...] = alpha * acc_ref[...] + pv
    m_ref[...] = m_new


def _paged_copies(pt_ref, srcs, dsts, sem, b, c, slot, *, pps, n_pages, layer, start):
    base = b * n_pages + c * pps
    for j in range(pps):
        pg = pt_ref[base + j]
        for a, (src, dst) in enumerate(zip(srcs, dsts)):
            cp = pltpu.make_async_copy(src.at[layer, pg], dst(slot, j), sem.at[a, slot])
            if start:
                cp.start(priority=a)
            else:
                cp.wait()


def _paged_pipeline(fetch):
    b, c = pl.program_id(0), pl.program_id(1)
    nb, nc = pl.num_programs(0), pl.num_programs(1)
    step = b * nc + c
    slot = step % 2

    @pl.when(step == 0)
    def _():
        fetch(b, c, slot, start=True)

    @pl.when(step + 1 < nb * nc)
    def _():
        wrap = c + 1 == nc
        fetch(jnp.where(wrap, b + 1, b), jnp.where(wrap, 0, c + 1), 1 - slot, start=True)

    fetch(b, c, slot, start=False)
    return slot


def _decode_mla_kernel(pt_ref, q_ref, knew_ref, ckv_hbm, krt_hbm, o_ref,
                       kbuf, rbuf, sem, m_ref, l_ref, acc_ref, *, pps, n_pages, layer, t_len):
    c = pl.program_id(1)
    dsts = [lambda s, j: kbuf.at[s, pl.ds(j * PAGE, PAGE), :],
            lambda s, j: rbuf.at[s, :, pl.ds(j * PAGE, PAGE)]]
    slot = _paged_pipeline(functools.partial(_paged_copies, pt_ref, [ckv_hbm, krt_hbm], dsts, sem,
                                             pps=pps, n_pages=n_pages, layer=layer))

    @pl.when(c == 0)
    def _():
        m_ref[...] = jnp.full(m_ref.shape, NEG_INF, F32)
        l_ref[...] = jnp.zeros(l_ref.shape, F32)
        acc_ref[...] = jnp.zeros(acc_ref.shape, F32)

    q = q_ref[0]
    kb = kbuf[slot].astype(BF16)
    s = _dot_nt(q[:, :LANE], kb) + _dot(q[:, LANE:LANE + MLA_ROPE], rbuf[slot].astype(BF16))
    _online_update(s, [kb], m_ref, l_ref, acc_ref)

    @pl.when(c == pl.num_programs(1) - 1)
    def _():
        kn = knew_ref[0]
        sn = _dot_nt(q, kn)
        t = lax.broadcasted_iota(jnp.int32, sn.shape, 0) % t_len
        j = lax.broadcasted_iota(jnp.int32, sn.shape, 1)
        _online_update(jnp.where(j <= t, sn, NEG_INF), [kn[:, :LANE]], m_ref, l_ref, acc_ref)
        o_ref[0] = (acc_ref[...] / l_ref[...]).astype(BF16)


def _decode_mla(page_table, q_rows, k_new, cache_ckv, cache_krt, layer, t_len, pps):
    db, rows, _ = q_rows.shape
    n_pages = page_table.shape[1]
    per_b = lambda shape: pl.BlockSpec(shape, lambda b, c, pt_ref: (b, 0, 0))
    hbm = pl.BlockSpec(memory_space=pl.ANY)
    grid_spec = pltpu.PrefetchScalarGridSpec(
        num_scalar_prefetch=1,
        grid=(db, n_pages // pps),
        in_specs=[per_b((1, rows, QCAT)), per_b((1, PAGE, QCAT)), hbm, hbm],
        out_specs=per_b((1, rows, LANE)),
        scratch_shapes=[pltpu.VMEM((2, pps * PAGE, MLA_KV_RANK), F32),
                        pltpu.VMEM((2, MLA_ROPE, pps * PAGE), F32),
                        pltpu.SemaphoreType.DMA((2, 2)),
                        pltpu.VMEM((rows, 1), F32), pltpu.VMEM((rows, 1), F32),
                        pltpu.VMEM((rows, LANE), F32)],
    )
    return pl.pallas_call(
        functools.partial(_decode_mla_kernel, pps=pps, n_pages=n_pages, layer=layer, t_len=t_len),
        grid_spec=grid_spec,
        out_shape=jax.ShapeDtypeStruct((db, rows, LANE), BF16),
        compiler_params=_cparams(("arbitrary", "arbitrary"), 40),
        name="decode_mla",
    )(page_table.reshape(-1), q_rows, k_new, cache_ckv, cache_krt)


def _decode_diff_kernel(pt_ref, q_ref, knew_ref, vnew_ref, lq1_ref, lk1_ref, lq2_ref, lk2_ref, ng_ref,
                        k_hbm, v_hbm, o_ref, kbuf, vbuf, sem, qs_ref, m_ref, l_ref, acc_ref,
                        *, pps, n_pages, layer, t_len, lam_init):
    c = pl.program_id(1)
    half = DIFF_GROUP * t_len
    per_kh = 2 * half
    prow = DIFF_KVH * PAGE
    dsts = [lambda s, j: kbuf.at[s, pl.ds(j * prow, prow), :],
            lambda s, j: vbuf.at[s, pl.ds(j * prow, prow), :]]
    slot = _paged_pipeline(functools.partial(_paged_copies, pt_ref, [k_hbm, v_hbm], dsts, sem,
                                             pps=pps, n_pages=n_pages, layer=layer))

    @pl.when(c == 0)
    def _():
        m_ref[...] = jnp.full(m_ref.shape, NEG_INF, F32)
        l_ref[...] = jnp.zeros(l_ref.shape, F32)
        acc_ref[...] = jnp.zeros(acc_ref.shape, F32)
        for kh in range(DIFF_KVH):
            qa, qb = _diff_mask_rows(q_ref[0, kh * half:(kh + 1) * half, :])
            qs_ref[kh * per_kh:kh * per_kh + half, :] = qa
            qs_ref[kh * per_kh + half:(kh + 1) * per_kh, :] = qb

    def masked_scores(kt, new_tokens):
        s = _dot_nt(qs_ref[...], kt)
        row = lax.broadcasted_iota(jnp.int32, s.shape, 0)
        col = lax.broadcasted_iota(jnp.int32, s.shape, 1)
        keep = (col % DIFF_KVH) == (row // per_kh)
        if new_tokens:
            keep = keep & ((col // DIFF_KVH) <= (row % t_len))
        return jnp.where(keep, s, NEG_INF)

    def update(kt, vt, new_tokens):
        _online_update(masked_scores(kt, new_tokens), [vt], m_ref, l_ref, acc_ref)

    n = pps * prow
    halves = [slice(0, n // 2), slice(n // 2, n)]
    ss = [masked_scores(kbuf[slot, h, :].astype(BF16), False) for h in halves]
    for h, sh in zip(halves, ss):
        _online_update(sh, [vbuf[slot, h, :].astype(BF16)], m_ref, l_ref, acc_ref)

    @pl.when(c == pl.num_programs(1) - 1)
    def _():
        update(knew_ref[0], vnew_ref[0], True)
        lam = _diff_lambda(lq1_ref, lk1_ref, lq2_ref, lk2_ref, lam_init)
        for kh in range(DIFF_KVH):
            r1 = slice(kh * per_kh, kh * per_kh + half)
            r2 = slice(kh * per_kh + half, (kh + 1) * per_kh)
            o = acc_ref[r1, :] / l_ref[r1, :] - lam * (acc_ref[r2, :] / l_ref[r2, :])
            o = _rms(o, ng_ref[...]) * (1.0 - lam_init)
            o_ref[0, kh * half:(kh + 1) * half, :] = o.astype(BF16)


def _decode_diff(page_table, q_rows, k_new, v_new, lam_vecs, norm_g, cache_k, cache_v,
                 layer, t_len, lam_init, pps):
    db, rows, _ = q_rows.shape
    n_pages = page_table.shape[1]
    prow = DIFF_KVH * PAGE
    const = lambda shape: pl.BlockSpec(shape, lambda b, c, pt_ref: (0,) * len(shape))
    per_b = lambda shape: pl.BlockSpec(shape, lambda b, c, pt_ref: (b, 0, 0))
    hbm = pl.BlockSpec(memory_space=pl.ANY)
    grid_spec = pltpu.PrefetchScalarGridSpec(
        num_scalar_prefetch=1,
        grid=(db, n_pages // pps),
        in_specs=[per_b((1, rows, LANE)), per_b((1, prow, LANE)), per_b((1, prow, LANE)),
                  const((1, DIFF_HD)), const((1, DIFF_HD)), const((1, DIFF_HD)), const((1, DIFF_HD)),
                  const((1, LANE)), hbm, hbm],
        out_specs=per_b((1, rows, LANE)),
        scratch_shapes=[pltpu.VMEM((2, pps * prow, LANE), F32), pltpu.VMEM((2, pps * prow, LANE), F32),
                        pltpu.SemaphoreType.DMA((2, 2)),
                        pltpu.VMEM((2 * rows, LANE), BF16), pltpu.VMEM((2 * rows, 1), F32),
                        pltpu.VMEM((2 * rows, 1), F32), pltpu.VMEM((2 * rows, LANE), F32)],
    )
    return pl.pallas_call(
        functools.partial(_decode_diff_kernel, pps=pps, n_pages=n_pages, layer=layer, t_len=t_len,
                          lam_init=lam_init),
        grid_spec=grid_spec,
        out_shape=jax.ShapeDtypeStruct((db, rows, LANE), BF16),
        compiler_params=_cparams(("arbitrary", "arbitrary"), 56),
        name="decode_diff",
    )(page_table.reshape(-1), q_rows, k_new, v_new, *lam_vecs, norm_g, cache_k, cache_v)


def _odd_in_kernel(x_ref, w_ref, q_ref, k_ref, v_ref, kb_ref, vb_ref, vt_ref):
    h = _dot(x_ref[...].astype(BF16), w_ref[...])
    nq = DIFF_HEADS * 2 * DIFF_HD
    nkv = DIFF_KVH * 2 * DIFF_HD
    q_ref[...] = (h[:, :nq] * (DIFF_SCALE * LOG2E)).astype(BF16)
    k = h[:, nq:nq + nkv]
    v = h[:, nq + nkv:]
    tm = k.shape[0]
    for kh in range(DIFF_KVH):
        k_ref[pl.ds(kh, tm, stride=DIFF_KVH), :] = k[:, kh * LANE:(kh + 1) * LANE]
        v_ref[pl.ds(kh, tm, stride=DIFF_KVH), :] = v[:, kh * LANE:(kh + 1) * LANE]
        vt_ref[0, kh, 0] = v[:, kh * LANE:(kh + 1) * LANE].T.astype(BF16)
    kb_ref[...] = k.astype(BF16)
    vb_ref[...] = v.astype(BF16)


def _odd_in(x2d, w_in, tm, nblk):
    m = x2d.shape[0]
    nq = DIFF_HEADS * 2 * DIFF_HD
    nkv = DIFF_KVH * 2 * DIFF_HD
    row = lambda w: pl.BlockSpec((tm, w), lambda i: (i, 0))
    return pl.pallas_call(
        _odd_in_kernel,
        grid=(m // tm,),
        in_specs=[row(D_MODEL), _const_spec((D_MODEL, nq + 2 * nkv))],
        out_specs=[row(nq), pl.BlockSpec((DIFF_KVH * tm, LANE), lambda i: (i, 0)),
                   pl.BlockSpec((DIFF_KVH * tm, LANE), lambda i: (i, 0)), row(nkv), row(nkv),
                   pl.BlockSpec((1, DIFF_KVH, 1, LANE, tm), lambda i: (i // nblk, 0, i % nblk, 0, 0))],
        out_shape=[jax.ShapeDtypeStruct((m, nq), BF16),
                   jax.ShapeDtypeStruct((DIFF_KVH * m, LANE), F32), jax.ShapeDtypeStruct((DIFF_KVH * m, LANE), F32),
                   jax.ShapeDtypeStruct((m, nkv), BF16), jax.ShapeDtypeStruct((m, nkv), BF16),
                   jax.ShapeDtypeStruct((m // (nblk * tm), DIFF_KVH, nblk, LANE, tm), BF16)],
        compiler_params=_cparams(("arbitrary",), 40),
        name="odd_in",
    )(x2d, w_in)


def _post_kernel(*refs, even):
    if even:
        (x_ref, a_ref, pool_ref, wuv_ref, wmix_ref, g1_ref, b1_ref, g2_ref, b2_ref,
         w1_ref, fb1_ref, w2_ref, fb2_ref, o_ref) = refs
        o = _dot(a_ref[...], wuv_ref[...]).astype(BF16)
        mix = _dot(pool_ref[...], wmix_ref[:POOL_DIM, :]) + _dot(o, wmix_ref[POOL_DIM:, :])
    else:
        (x_ref, a_ref, wmix_ref, g1_ref, b1_ref, g2_ref, b2_ref,
         w1_ref, fb1_ref, w2_ref, fb2_ref, o_ref) = refs
        mix = _dot(a_ref[...], wmix_ref[...])
    x1 = _ln(ALPHA * x_ref[...] + mix, g1_ref[...], b1_ref[...])
    x1b = x1.astype(BF16)
    ffn = None
    for c in range(D_FF // FF_CHUNK):
        cols = slice(c * FF_CHUNK, (c + 1) * FF_CHUNK)
        hdn = jnp.square(jnp.maximum(_dot(x1b, w1_ref[:, cols]) + fb1_ref[:, cols], 0.0))
        part = _dot(hdn.astype(BF16), w2_ref[cols, :])
        ffn = part if ffn is None else ffn + part
    o_ref[...] = _ln(ALPHA * x1 + (ffn + fb2_ref[...]), g2_ref[...], b2_ref[...])


def _post(x2d, a, extra, wts, even, tm):
    m = x2d.shape[0]
    row = lambda w: pl.BlockSpec((tm, w), lambda i: (i, 0))
    vec = _const_spec((1, D_MODEL))
    ins = [x2d, a]
    specs = [row(D_MODEL), row(a.shape[1])]
    if even:
        ins += [extra, wts["w_uv"]]
        specs += [row(POOL_DIM), _const_spec((MLA_HEADS * LANE, MLA_HEADS * MLA_V))]
    ins += [wts["w_mix"], wts["g1"], wts["b1"], wts["g2"], wts["b2"],
            wts["w1"], wts["fb1"], wts["w2"], wts["fb2"]]
    specs += [_const_spec((D_MODEL, D_MODEL)), vec, vec, vec, vec,
              _const_spec((D_MODEL, D_FF)), _const_spec((1, D_FF)), _const_spec((D_FF, D_MODEL)), vec]
    return pl.pallas_call(
        functools.partial(_post_kernel, even=even),
        grid=(m // tm,),
        in_specs=specs,
        out_specs=row(D_MODEL),
        out_shape=jax.ShapeDtypeStruct((m, D_MODEL), F32),
        compiler_params=_cparams(("arbitrary",), 56),
        name="post_even" if even else "post_odd",
    )(*ins)


def _rope_tables(pos):
    half = MLA_ROPE // 2
    freq = ROPE_THETA ** (-jnp.arange(half, dtype=F32) / half)
    ang = pos[:, None] * freq[None, :]
    cos, sin = jnp.cos(ang), jnp.sin(ang)
    cc = jnp.concatenate([cos, cos], -1)
    ss = jnp.concatenate([-sin, sin], -1)
    pad = jnp.zeros((pos.shape[0], LANE - MLA_ROPE), F32)
    return (jnp.tile(cc, (1, MLA_HEADS)), jnp.tile(ss, (1, MLA_HEADS)),
            jnp.concatenate([cc, pad], -1), jnp.concatenate([ss, pad], -1))


def _block_diag(blocks):
    h, r, c = blocks.shape
    eye = jnp.eye(h, dtype=blocks.dtype)
    return (blocks[:, :, None, :] * eye[:, None, :, None]).reshape(h * r, h * c)


def _even_weights(w_in, q_norm, w_uq, kv_norm, w_uk, w_uv, w_out):
    half = MLA_ROPE // 2
    kr0 = POOL_DIM + MLA_Q_RANK + MLA_KV_RANK
    zpad = jnp.zeros((D_MODEL, LANE - MLA_ROPE), F32)
    w_in_p = jnp.concatenate(
        [w_in, zpad, w_in[:, kr0 + half:], w_in[:, kr0:kr0 + half], zpad], -1).astype(BF16)
    uq = w_uq.reshape(MLA_Q_RANK, MLA_HEADS, MLA_NOPE + MLA_ROPE)
    nope = uq[:, :, :MLA_NOPE].reshape(MLA_Q_RANK, MLA_HEADS * MLA_NOPE)
    rope = uq[:, :, MLA_NOPE:]
    rope_a = rope.reshape(MLA_Q_RANK, MLA_HEADS * MLA_ROPE)
    rope_b = jnp.concatenate([rope[:, :, half:], rope[:, :, :half]], -1).reshape(MLA_Q_RANK, MLA_HEADS * MLA_ROPE)
    place = jnp.concatenate([jnp.eye(MLA_ROPE, dtype=F32), jnp.zeros((MLA_ROPE, LANE - MLA_ROPE), F32)], -1)
    return {
        "w_in": w_in_p,
        "q_norm": q_norm.reshape(1, -1),
        "w_uq": jnp.concatenate([nope, rope_a, rope_b], -1).astype(BF16),
        "kv_norm": kv_norm.reshape(1, -1),
        "w_bd": _block_diag(jnp.transpose(w_uk, (1, 2, 0))).astype(BF16),
        "place": _block_diag(jnp.broadcast_to(place, (MLA_HEADS,) + place.shape)).astype(BF16),
        "w_uv": _block_diag(jnp.transpose(w_uv, (1, 0, 2))).astype(BF16),
        "w_mix": w_out.astype(BF16),
    }


def _ffn_weights(g1, b1, g2, b2, w1, fb1, w2, fb2):
    r = lambda v: v.reshape(1, -1)
    return {"g1": r(g1), "b1": r(b1), "g2": r(g2), "b2": r(b2),
            "w1": w1.astype(BF16), "fb1": r(fb1), "w2": w2.astype(BF16), "fb2": r(fb2)}


def _token_tile(m):
    return TOKEN_TILE if m % TOKEN_TILE == 0 else m


def kernel(x_prompt, x_sample, cache_mla_ckv, cache_mla_krope, cache_diff_k, cache_diff_v, state_pool,
           page_table, w_in_even, pool_w, pool_scale, mla_q_norm, mla_w_uq, mla_kv_norm, mla_w_uk,
           mla_w_uv, w_out_even, w_in_odd, diff_lq1, diff_lk1, diff_lq2, diff_lk2, diff_norm, w_out_odd,
           ln1_g, ln1_b, ln2_g, ln2_b, mlp_w1, mlp_b1, mlp_w2, mlp_b2):
    bsz, s_len, d = x_prompt.shape
    db, t_len, _ = x_sample.shape
    n_pages = page_table.shape[1]
    past = n_pages * PAGE
    mp, ms = bsz * s_len, db * t_len
    tmp, tms = _token_tile(mp), _token_tile(ms)
    tq = tmp
    pps_mla = min(MLA_PAGES_PER_STEP, n_pages)
    pps_diff = min(DIFF_PAGES_PER_STEP, n_pages)
    n_pool = cache_diff_k.shape[1]
    kvw = DIFF_KVH * 2 * DIFF_HD
    cache_dk = cache_diff_k.reshape(cache_diff_k.shape[0], n_pool, DIFF_KVH * PAGE, LANE)
    cache_dv = cache_diff_v.reshape(cache_diff_v.shape[0], n_pool, DIFF_KVH * PAGE, LANE)
    cache_krt = jnp.swapaxes(cache_mla_krope, 2, 3)

    tabs_p = _rope_tables(jnp.arange(s_len, dtype=F32))
    tabs_s = tuple(jnp.tile(t, (tms // t_len, 1)) for t in _rope_tables(past + jnp.arange(t_len, dtype=F32)))
    pad_rows = lambda a, n: jnp.pad(a, ((0, 0), (0, n - a.shape[1]), (0, 0)))
    hist = 24 - t_len

    xp = x_prompt.reshape(mp, d)
    xs = x_sample.reshape(ms, d)
    outs = {k: [] for k in ("p_ckv", "p_kr", "p_dk", "p_dv", "p_pool", "s_ckv", "s_kr", "s_dk", "s_dv", "s_pool")}
    for i in range(DEPTH):
        ffn = _ffn_weights(ln1_g[i], ln1_b[i], ln2_g[i], ln2_b[i], mlp_w1[i], mlp_b1[i], mlp_w2[i], mlp_b2[i])
        if i % 2 == 0:
            e = i // 2
            wts = _even_weights(w_in_even[e], mla_q_norm[e], mla_w_uq[e], mla_kv_norm[e], mla_w_uk[e],
                                mla_w_uv[e], w_out_even[e])
            wts.update(ffn)
            pw = pool_w[e].astype(BF16)
            psc = pool_scale[e].reshape(1, -1)
            u, qcat, kcat, ckv, kr, vt = _even_in(xp, wts, tabs_p, s_len // tmp, tmp)
            u3 = u.reshape(bsz, s_len, POOL_DIM)
            pool_o = _pool(u3, pw, psc, 0, 1).reshape(mp, POOL_DIM)
            o_lat = _flash_mla(qcat.reshape(bsz, s_len, -1), kcat.reshape(bsz, s_len, QCAT), vt, tq)
            xp = _post(xp, o_lat.reshape(mp, -1), pool_o, wts, True, tmp)
            outs["p_ckv"].append(ckv.reshape(bsz, s_len, -1))
            outs["p_kr"].append(kr.reshape(bsz, s_len, -1))
            outs["p_pool"].append(u3[:, s_len - POOL_STATE:])
            u, qcat, kcat, ckv, kr, _ = _even_in(xs, wts, tabs_s, 1, tms)
            u3 = u.reshape(db, t_len, POOL_DIM)
            u_ext = jnp.concatenate([state_pool[e], u3], axis=1)
            u_pad = jnp.pad(u_ext, ((0, 0), (24 - u_ext.shape[1], 0), (0, 0)))
            bb = db
            while bb * u_pad.shape[1] > 2048 and bb % 2 == 0:
                bb //= 2
            pool_o = _pool(u_pad, pw, psc, past - hist, bb)[:, hist:].reshape(ms, POOL_DIM)
            q_rows = jnp.transpose(qcat.reshape(db, t_len, MLA_HEADS, QCAT), (0, 2, 1, 3))
            o_rows = _decode_mla(page_table, q_rows.reshape(db, MLA_HEADS * t_len, QCAT),
                                 pad_rows(kcat.reshape(db, t_len, QCAT), PAGE),
                                 cache_mla_ckv, cache_krt, e, t_len, pps_mla)
            o_lat = jnp.transpose(o_rows.reshape(db, MLA_HEADS, t_len, LANE), (0, 2, 1, 3)).reshape(ms, -1)
            xs = _post(xs, o_lat, pool_o, wts, True, tms)
            outs["s_ckv"].append(ckv.reshape(db, t_len, -1))
            outs["s_kr"].append(kr.reshape(db, t_len, -1))
            outs["s_pool"].append(u_ext[:, u_ext.shape[1] - POOL_STATE:])
        else:
            o = i // 2
            lam_init = 0.8 - 0.6 * math.exp(-0.3 * i)
            lam_vecs = tuple(v[o].reshape(1, -1) for v in (diff_lq1, diff_lk1, diff_lq2, diff_lk2))
            norm_g = diff_norm[o].reshape(1, -1)
            wts = dict(ffn)
            wts["w_mix"] = w_out_odd[o].astype(BF16)
            w_in = w_in_odd[o].astype(BF16)
            q, k, v, kb, _, vt = _odd_in(xp, w_in, tmp, s_len // tmp)
            att = _flash_diff(q.reshape(bsz, s_len, -1), kb.reshape(bsz, s_len, kvw), vt,
                              lam_vecs, norm_g.reshape(-1, 1), lam_init, tq)
            xp = _post(xp, att.reshape(mp, -1), None, wts, False, tmp)
            outs["p_dk"].append(k.reshape(bsz, s_len, DIFF_KVH, 2 * DIFF_HD))
            outs["p_dv"].append(v.reshape(bsz, s_len, DIFF_KVH, 2 * DIFF_HD))
            q, k, v, kb, vb, _ = _odd_in(xs, w_in, tms, 1)
            heads = DIFF_KVH * DIFF_GROUP
            q_rows = jnp.transpose(q.reshape(db, t_len, heads, LANE), (0, 2, 1, 3)).reshape(db, heads * t_len, LANE)
            new_rows = lambda a: pad_rows(a.reshape(db, t_len * DIFF_KVH, LANE), DIFF_KVH * PAGE)
            o_rows = _decode_diff(page_table, q_rows, new_rows(kb), new_rows(vb), lam_vecs, norm_g,
                                  cache_dk, cache_dv, o, t_len, lam_init, pps_diff)
            att = jnp.transpose(o_rows.reshape(db, heads, t_len, LANE), (0, 2, 1, 3)).reshape(ms, -1)
            xs = _post(xs, att, None, wts, False, tms)
            outs["s_dk"].append(k.reshape(db, t_len, DIFF_KVH, 2 * DIFF_HD))
            outs["s_dv"].append(v.reshape(db, t_len, DIFF_KVH, 2 * DIFF_HD))

    st = lambda name: jnp.stack(outs[name])
    return (xp.reshape(bsz, s_len, d), xs.reshape(db, t_len, d),
            st("p_ckv"), st("p_kr"), st("p_dk"), st("p_dv"), st("p_pool"),
            st("s_ckv"), st("s_kr"), st("s_dk"), st("s_dv"), st("s_pool"))
```

```python
import functools
import math

import jax
import jax.numpy as jnp
from jax import lax
from jax.experimental import pallas as pl
from jax.experimental.pallas import tpu as pltpu

F32 = jnp.float32
BF16 = jnp.bfloat16

D_MODEL = 1024
DEPTH = 4
PAGE = 128
POOL_DIM = 512
POOL_WINDOWS = (2, 4, 8, 16)
POOL_GD = 128
POOL_PAD = 16
POOL_STATE = 15
MLA_HEADS = 8
MLA_NOPE = 64
MLA_ROPE = 32
MLA_V = 64
MLA_Q_RANK = 256
MLA_KV_RANK = 128
MLA_SCALE = (MLA_NOPE + MLA_ROPE) ** -0.5
LOG2E = math.log2(math.e)
ROPE_THETA = 10000.0
DIFF_HEADS = 8
DIFF_HD = 64
DIFF_KVH = 2
DIFF_GROUP = 4
DIFF_SCALE = DIFF_HD ** -0.5
D_FF = 4096
ALPHA = (2 * DEPTH) ** 0.25
EPS = 1e-5
QCAT = 256
EVEN_IN_P = 1152

LANE = 128
TOKEN_TILE = 512
FF_CHUNK = 1024
MLA_PAGES_PER_STEP = 64
DIFF_PAGES_PER_STEP = 32
NEG_INF = float("-inf")


def _cparams(sem, vmem_mib):
    return pltpu.CompilerParams(dimension_semantics=sem, vmem_limit_bytes=vmem_mib * 1024 * 1024)


def _const_spec(shape):
    nd = len(shape)
    return pl.BlockSpec(shape, lambda *_: (0,) * nd, pipeline_mode=pl.Buffered(1))


def _dot(a, b):
    return jnp.dot(a, b, preferred_element_type=F32)


def _dot_nt(a, b):
    return lax.dot_general(a, b, (((1,), (1,)), ((), ())), preferred_element_type=F32)


def _rms(x, g):
    return x * lax.rsqrt(jnp.mean(x * x, -1, keepdims=True) + EPS) * g


def _ln(x, g, b):
    mu = jnp.mean(x, -1, keepdims=True)
    xc = x - mu
    var = jnp.mean(xc * xc, -1, keepdims=True)
    return xc * lax.rsqrt(var + EPS) * g + b


def _even_in_kernel(x_ref, w_in_ref, qn_ref, wuq_ref, kvn_ref, wbd_ref, place_ref,
                    cq_ref, sq_ref, ck_ref, sk_ref,
                    u_ref, qcat_ref, kcat_ref, ckv_ref, kr_ref, vt_ref):
    h = _dot(x_ref[...].astype(BF16), w_in_ref[...])
    u_ref[...] = h[:, :POOL_DIM]
    cqn = _rms(h[:, 512:768], qn_ref[...])
    q = _dot(cqn.astype(BF16), wuq_ref[...])
    rot = (q[:, 512:768] * cq_ref[...] + q[:, 768:1024] * sq_ref[...]) * (MLA_SCALE * LOG2E)
    qlat = _dot(q[:, :512].astype(BF16), wbd_ref[...]) * (MLA_SCALE * LOG2E)
    qrope = _dot(rot.astype(BF16), place_ref[...])
    for hh in range(MLA_HEADS):
        qcat_ref[:, hh * QCAT:hh * QCAT + LANE] = qlat[:, hh * LANE:(hh + 1) * LANE].astype(BF16)
        qcat_ref[:, hh * QCAT + LANE:(hh + 1) * QCAT] = qrope[:, hh * LANE:(hh + 1) * LANE].astype(BF16)
    ckvn = _rms(h[:, 768:896], kvn_ref[...])
    ckv_ref[...] = ckvn
    krr = h[:, 896:1024] * ck_ref[...] + h[:, 1024:1152] * sk_ref[...]
    kr_ref[...] = krr[:, :MLA_ROPE]
    kcat_ref[:, :LANE] = ckvn.astype(BF16)
    kcat_ref[:, LANE:] = krr.astype(BF16)
    vt_ref[0, 0] = ckvn.T.astype(BF16)


def _even_in(x2d, wts, tabs, n_tab_blocks, tm):
    m = x2d.shape[0]
    nblk = max(n_tab_blocks, 1)
    row = lambda w: pl.BlockSpec((tm, w), lambda i: (i, 0))
    tab = lambda w: pl.BlockSpec((tm, w), lambda i: (i % n_tab_blocks, 0))
    return pl.pallas_call(
        _even_in_kernel,
        grid=(m // tm,),
        in_specs=[row(D_MODEL), _const_spec((D_MODEL, EVEN_IN_P)), _const_spec((1, MLA_Q_RANK)),
                  _const_spec((MLA_Q_RANK, 1024)), _const_spec((1, MLA_KV_RANK)),
                  _const_spec((512, 1024)), _const_spec((256, 1024)),
                  tab(256), tab(256), tab(LANE), tab(LANE)],
        out_specs=[row(POOL_DIM), row(MLA_HEADS * QCAT), row(QCAT), row(MLA_KV_RANK), row(MLA_ROPE),
                   pl.BlockSpec((1, 1, LANE, tm), lambda i: (i // nblk, i % nblk, 0, 0))],
        out_shape=[jax.ShapeDtypeStruct((m, POOL_DIM), F32),
                   jax.ShapeDtypeStruct((m, MLA_HEADS * QCAT), BF16),
                   jax.ShapeDtypeStruct((m, QCAT), BF16),
                   jax.ShapeDtypeStruct((m, MLA_KV_RANK), F32),
                   jax.ShapeDtypeStruct((m, MLA_ROPE), F32),
                   jax.ShapeDtypeStruct((m // (nblk * tm), nblk, LANE, tm), BF16)],
        compiler_params=_cparams(("arbitrary",), 40),
        name="even_in",
    )(x2d, wts["w_in"], wts["q_norm"], wts["w_uq"], wts["kv_norm"], wts["w_bd"], wts["place"], *tabs)


def _pool_kernel(u_ref, w_ref, sc_ref, o_ref, ext_ref, *, start_pos):
    bb, seq, _ = u_ref.shape
    ext_ref[:, :POOL_PAD, :] = jnp.zeros((bb, POOL_PAD, POOL_DIM), F32)
    ext_ref[:, POOL_PAD:, :] = u_ref[...]
    pos = start_pos + lax.broadcasted_iota(jnp.int32, (1, seq, 1), 1).astype(F32)
    for g, win in enumerate(POOL_WINDOWS):
        lanes = slice(g * POOL_GD, (g + 1) * POOL_GD)
        tot = ext_ref[:, POOL_PAD:POOL_PAD + seq, lanes]
        for j in range(1, win):
            tot = tot + ext_ref[:, POOL_PAD - j:POOL_PAD - j + seq, lanes]
        cnt = jnp.minimum(pos + 1.0, float(win))
        d = tot / cnt - u_ref[:, :, lanes]
        y = _dot(d.reshape(bb * seq, POOL_GD).astype(BF16), w_ref[g])
        o_ref[:, :, lanes] = (y.reshape(bb, seq, POOL_GD) * sc_ref[:, lanes]).astype(BF16)


def _pool(u_ext, pool_w, pool_scale, start_pos, bb):
    nb, seq, _ = u_ext.shape
    blk = pl.BlockSpec((bb, seq, POOL_DIM), lambda i: (i, 0, 0))
    return pl.pallas_call(
        functools.partial(_pool_kernel, start_pos=float(start_pos)),
        grid=(nb // bb,),
        in_specs=[blk, _const_spec((len(POOL_WINDOWS), POOL_GD, POOL_GD)), _const_spec((1, POOL_DIM))],
        out_specs=blk,
        out_shape=jax.ShapeDtypeStruct((nb, seq, POOL_DIM), BF16),
        scratch_shapes=[pltpu.VMEM((bb, POOL_PAD + seq, POOL_DIM), F32)],
        compiler_params=_cparams(("arbitrary",), 40),
        name="pool_mix",
    )(u_ext, pool_w, pool_scale)


def _flash_sweep(qs_ref, k_ref, vt_ref, m_ref, l_ref, acc_ref, st_ref, mx_ref, *, q_idx, tq, groups):
    rows = groups * tq
    m_ref[...] = jnp.full((1, rows), NEG_INF, F32)
    l_ref[...] = jnp.zeros((1, rows), F32)
    acc_ref[...] = jnp.zeros((LANE, rows), F32)
    q = qs_ref[...]

    def scores(j, diagonal):
        start = pl.multiple_of(j * tq, tq)
        st = _dot_nt(k_ref[pl.ds(start, tq), :], q)
        if diagonal:
            key = lax.broadcasted_iota(jnp.int32, (tq, rows), 0)
            t = lax.broadcasted_iota(jnp.int32, (tq, rows), 1) % tq
            st = jnp.where(key <= t, st, NEG_INF)
        return st, jnp.max(st, 0, keepdims=True)

    def consume(j, st, mx):
        m_old = m_ref[...]
        m_new = jnp.maximum(m_old, mx)
        p = jnp.exp2(st - m_new)
        alpha = jnp.exp2(m_old - m_new)
        l_ref[...] = alpha * l_ref[...] + jnp.sum(p, 0, keepdims=True)
        acc_ref[...] = alpha * acc_ref[...] + _dot(vt_ref[j], p.astype(BF16))
        m_ref[...] = m_new

    @pl.when(q_idx == 0)
    def _():
        consume(0, *scores(0, True))

    @pl.when(q_idx > 0)
    def _():
        st_ref[...], mx_ref[...] = scores(0, False)

        def body(j, carry):
            nxt, nmx = scores(j + 1, False)
            consume(j, st_ref[...], mx_ref[...])
            st_ref[...] = nxt
            mx_ref[...] = nmx
            return carry

        lax.fori_loop(0, q_idx - 1, body, 0)
        nxt, nmx = scores(q_idx, True)
        consume(q_idx - 1, st_ref[...], mx_ref[...])
        consume(q_idx, nxt, nmx)


def _flash_scratch(rows, dk, tq):
    return [pltpu.VMEM((rows, dk), BF16), pltpu.VMEM((1, rows), F32), pltpu.VMEM((1, rows), F32),
            pltpu.VMEM((LANE, rows), F32), pltpu.VMEM((tq, rows), F32), pltpu.VMEM((1, rows), F32)]


def _flash_mla_kernel(q_ref, k_ref, vt_ref, o_ref, qs_ref, m_ref, l_ref, acc_ref, st_ref, mx_ref, *, tq):
    for hh in range(MLA_HEADS):
        qs_ref[hh * tq:(hh + 1) * tq, :] = q_ref[0, :, hh * QCAT:(hh + 1) * QCAT]
    _flash_sweep(qs_ref, k_ref.at[0], vt_ref.at[0], m_ref, l_ref, acc_ref, st_ref, mx_ref,
                 q_idx=pl.program_id(1), tq=tq, groups=MLA_HEADS)
    for hh in range(MLA_HEADS):
        cols = slice(hh * tq, (hh + 1) * tq)
        o = acc_ref[:, cols] * (1.0 / l_ref[:, cols])
        o_ref[0, :, hh * LANE:(hh + 1) * LANE] = o.T.astype(BF16)


def _flash_mla(qcat, kcat, vt, tq):
    b, s, _ = qcat.shape
    return pl.pallas_call(
        functools.partial(_flash_mla_kernel, tq=tq),
        grid=(b, s // tq),
        in_specs=[pl.BlockSpec((1, tq, MLA_HEADS * QCAT), lambda bi, i: (bi, i, 0)),
                  pl.BlockSpec((1, s, QCAT), lambda bi, i: (bi, 0, 0)),
                  pl.BlockSpec((1, s // tq, LANE, tq), lambda bi, i: (bi, 0, 0, 0))],
        out_specs=pl.BlockSpec((1, tq, MLA_HEADS * LANE), lambda bi, i: (bi, i, 0)),
        out_shape=jax.ShapeDtypeStruct((b, s, MLA_HEADS * LANE), BF16),
        scratch_shapes=_flash_scratch(MLA_HEADS * tq, QCAT, tq),
        compiler_params=_cparams(("arbitrary", "arbitrary"), 60),
        name="flash_mla",
    )(qcat, kcat, vt)


def _diff_lambda(lq1_ref, lk1_ref, lq2_ref, lk2_ref, lam_init):
    a = jnp.sum(lq1_ref[...] * lk1_ref[...], -1, keepdims=True)
    b = jnp.sum(lq2_ref[...] * lk2_ref[...], -1, keepdims=True)
    return jnp.exp(a) - jnp.exp(b) + lam_init


def _diff_mask_rows(tile):
    lane = lax.broadcasted_iota(jnp.int32, tile.shape, 1)
    zero = jnp.zeros_like(tile)
    return jnp.where(lane < DIFF_HD, tile, zero), jnp.where(lane >= DIFF_HD, tile, zero)


def _flash_diff_kernel(q_ref, k_ref, vt_ref, lq1_ref, lk1_ref, lq2_ref, lk2_ref, ngc_ref,
                       o_ref, qs_ref, m_ref, l_ref, acc_ref, st_ref, mx_ref, *, tq, lam_init):
    for g in range(DIFF_GROUP):
        qa, qb = _diff_mask_rows(q_ref[0, :, g * LANE:(g + 1) * LANE])
        qs_ref[(2 * g) * tq:(2 * g + 1) * tq, :] = qa
        qs_ref[(2 * g + 1) * tq:(2 * g + 2) * tq, :] = qb
    _flash_sweep(qs_ref, k_ref.at[0], vt_ref.at[0, 0], m_ref, l_ref, acc_ref, st_ref, mx_ref,
                 q_idx=pl.program_id(2), tq=tq, groups=2 * DIFF_GROUP)
    lam = _diff_lambda(lq1_ref, lk1_ref, lq2_ref, lk2_ref, lam_init)
    for g in range(DIFF_GROUP):
        c1 = slice((2 * g) * tq, (2 * g + 1) * tq)
        c2 = slice((2 * g + 1) * tq, (2 * g + 2) * tq)
        o = acc_ref[:, c1] * (1.0 / l_ref[:, c1]) - acc_ref[:, c2] * (lam / l_ref[:, c2])
        o = o * lax.rsqrt(jnp.mean(o * o, 0, keepdims=True) + EPS) * ngc_ref[...] * (1.0 - lam_init)
        o_ref[0, :, g * LANE:(g + 1) * LANE] = o.T.astype(BF16)


def _flash_diff(q, k, vt, lam_vecs, norm_col, lam_init, tq):
    b, s, _ = q.shape
    gw = DIFF_GROUP * LANE
    vec = _const_spec((1, DIFF_HD))
    return pl.pallas_call(
        functools.partial(_flash_diff_kernel, tq=tq, lam_init=lam_init),
        grid=(b, DIFF_KVH, s // tq),
        in_specs=[pl.BlockSpec((1, tq, gw), lambda bi, kh, i: (bi, i, kh)),
                  pl.BlockSpec((1, s, LANE), lambda bi, kh, i: (bi, 0, kh)),
                  pl.BlockSpec((1, 1, s // tq, LANE, tq), lambda bi, kh, i: (bi, kh, 0, 0, 0)),
                  vec, vec, vec, vec, _const_spec((LANE, 1))],
        out_specs=pl.BlockSpec((1, tq, gw), lambda bi, kh, i: (bi, i, kh)),
        out_shape=jax.ShapeDtypeStruct((b, s, DIFF_KVH * gw), BF16),
        scratch_shapes=_flash_scratch(2 * DIFF_GROUP * tq, LANE, tq),
        compiler_params=_cparams(("arbitrary", "arbitrary", "arbitrary"), 60),
        name="flash_diff",
    )(q, k, vt, *lam_vecs, norm_col)


def _online_update(s, vals, m_ref, l_ref, acc_ref):
    w = s.shape[1] // len(vals)
    m_old = m_ref[...]
    m_new = jnp.maximum(m_old, jnp.max(s, -1, keepdims=True))
    p = jnp.exp2(s - m_new)
    alpha = jnp.exp2(m_old - m_new)
    l_ref[...] = alpha * l_ref[...] + jnp.sum(p, -1, keepdims=True)
    pv = None
    for j, vj in enumerate(vals):
        t = _dot(p[:, j * w:(j + 1) * w].astype(BF16), vj)
        pv = t if pv is None else pv + t
    acc_ref[...] = alpha * acc_ref[...] + pv
    m_ref[...] = m_new


def _paged_copies(pt_ref, srcs, dsts, sem, b, c, slot, *, pps, n_pages, layer, start):
    base = b * n_pages + c * pps
    for j in range(pps):
        pg = pt_ref[base + j]
        for a, (src, dst) in enumerate(zip(srcs, dsts)):
            cp = pltpu.make_async_copy(src.at[layer, pg], dst(slot, j), sem.at[a, slot])
            if start:
                cp.start()
            else:
                cp.wait()


def _paged_pipeline(fetch):
    b, c = pl.program_id(0), pl.program_id(1)
    nb, nc = pl.num_programs(0), pl.num_programs(1)
    step = b * nc + c
    slot = step % 2

    @pl.when(step == 0)
    def _():
        fetch(b, c, slot, start=True)

    @pl.when(step + 1 < nb * nc)
    def _():
        wrap = c + 1 == nc
        fetch(jnp.where(wrap, b + 1, b), jnp.where(wrap, 0, c + 1), 1 - slot, start=True)

    fetch(b, c, slot, start=False)
    return slot


def _decode_mla_kernel(pt_ref, q_ref, knew_ref, ckv_hbm, krt_hbm, o_ref,
                       kbuf, rbuf, sem, m_ref, l_ref, acc_ref, *, pps, n_pages, layer, t_len):
    c = pl.program_id(1)
    dsts = [lambda s, j: kbuf.at[s, pl.ds(j * PAGE, PAGE), :],
            lambda s, j: rbuf.at[s, :, pl.ds(j * PAGE, PAGE)]]
    slot = _paged_pipeline(functools.partial(_paged_copies, pt_ref, [ckv_hbm, krt_hbm], dsts, sem,
                                             pps=pps, n_pages=n_pages, layer=layer))

    @pl.when(c == 0)
    def _():
        m_ref[...] = jnp.full(m_ref.shape, NEG_INF, F32)
        l_ref[...] = jnp.zeros(l_ref.shape, F32)
        acc_ref[...] = jnp.zeros(acc_ref.shape, F32)

    q = q_ref[0]
    kb = kbuf[slot].astype(BF16)
    s = _dot_nt(q[:, :LANE], kb) + _dot(q[:, LANE:LANE + MLA_ROPE], rbuf[slot].astype(BF16))
    _online_update(s, [kb], m_ref, l_ref, acc_ref)

    @pl.when(c == pl.num_programs(1) - 1)
    def _():
        kn = knew_ref[0]
        sn = _dot_nt(q, kn)
        t = lax.broadcasted_iota(jnp.int32, sn.shape, 0) % t_len
        j = lax.broadcasted_iota(jnp.int32, sn.shape, 1)
        _online_update(jnp.where(j <= t, sn, NEG_INF), [kn[:, :LANE]], m_ref, l_ref, acc_ref)
        o_ref[0] = (acc_ref[...] / l_ref[...]).astype(BF16)


def _decode_mla(page_table, q_rows, k_new, cache_ckv, cache_krt, layer, t_len, pps):
    db, rows, _ = q_rows.shape
    n_pages = page_table.shape[1]
    per_b = lambda shape: pl.BlockSpec(shape, lambda b, c, pt_ref: (b, 0, 0))
    hbm = pl.BlockSpec(memory_space=pl.ANY)
    grid_spec = pltpu.PrefetchScalarGridSpec(
        num_scalar_prefetch=1,
        grid=(db, n_pages // pps),
        in_specs=[per_b((1, rows, QCAT)), per_b((1, PAGE, QCAT)), hbm, hbm],
        out_specs=per_b((1, rows, LANE)),
        scratch_shapes=[pltpu.VMEM((2, pps * PAGE, MLA_KV_RANK), F32),
                        pltpu.VMEM((2, MLA_ROPE, pps * PAGE), F32),
                        pltpu.SemaphoreType.DMA((2, 2)),
                        pltpu.VMEM((rows, 1), F32), pltpu.VMEM((rows, 1), F32),
                        pltpu.VMEM((rows, LANE), F32)],
    )
    return pl.pallas_call(
        functools.partial(_decode_mla_kernel, pps=pps, n_pages=n_pages, layer=layer, t_len=t_len),
        grid_spec=grid_spec,
        out_shape=jax.ShapeDtypeStruct((db, rows, LANE), BF16),
        compiler_params=_cparams(("arbitrary", "arbitrary"), 40),
        name="decode_mla",
    )(page_table.reshape(-1), q_rows, k_new, cache_ckv, cache_krt)


def _decode_diff_kernel(pt_ref, q_ref, knew_ref, vnew_ref, lq1_ref, lk1_ref, lq2_ref, lk2_ref, ng_ref,
                        k_hbm, v_hbm, o_ref, kbuf, vbuf, sem, qs_ref, m_ref, l_ref, acc_ref,
                        *, pps, n_pages, layer, t_len, lam_init):
    c = pl.program_id(1)
    half = DIFF_GROUP * t_len
    per_kh = 2 * half
    prow = DIFF_KVH * PAGE
    dsts = [lambda s, j: kbuf.at[s, pl.ds(j * prow, prow), :],
            lambda s, j: vbuf.at[s, pl.ds(j * prow, prow), :]]
    slot = _paged_pipeline(functools.partial(_paged_copies, pt_ref, [k_hbm, v_hbm], dsts, sem,
                                             pps=pps, n_pages=n_pages, layer=layer))

    @pl.when(c == 0)
    def _():
        m_ref[...] = jnp.full(m_ref.shape, NEG_INF, F32)
        l_ref[...] = jnp.zeros(l_ref.shape, F32)
        acc_ref[...] = jnp.zeros(acc_ref.shape, F32)
        for kh in range(DIFF_KVH):
            qa, qb = _diff_mask_rows(q_ref[0, kh * half:(kh + 1) * half, :])
            qs_ref[kh * per_kh:kh * per_kh + half, :] = qa
            qs_ref[kh * per_kh + half:(kh + 1) * per_kh, :] = qb

    def masked_scores(kt, new_tokens):
        s = _dot_nt(qs_ref[...], kt)
        row = lax.broadcasted_iota(jnp.int32, s.shape, 0)
        col = lax.broadcasted_iota(jnp.int32, s.shape, 1)
        keep = (col % DIFF_KVH) == (row // per_kh)
        if new_tokens:
            keep = keep & ((col // DIFF_KVH) <= (row % t_len))
        return jnp.where(keep, s, NEG_INF)

    def update(kt, vt, new_tokens):
        _online_update(masked_scores(kt, new_tokens), [vt], m_ref, l_ref, acc_ref)

    n = pps * prow
    halves = [slice(0, n // 2), slice(n // 2, n)]
    ss = [masked_scores(kbuf[slot, h, :].astype(BF16), False) for h in halves]
    for h, sh in zip(halves, ss):
        _online_update(sh, [vbuf[slot, h, :].astype(BF16)], m_ref, l_ref, acc_ref)

    @pl.when(c == pl.num_programs(1) - 1)
    def _():
        update(knew_ref[0], vnew_ref[0], True)
        lam = _diff_lambda(lq1_ref, lk1_ref, lq2_ref, lk2_ref, lam_init)
        for kh in range(DIFF_KVH):
            r1 = slice(kh * per_kh, kh * per_kh + half)
            r2 = slice(kh * per_kh + half, (kh + 1) * per_kh)
            o = acc_ref[r1, :] / l_ref[r1, :] - lam * (acc_ref[r2, :] / l_ref[r2, :])
            o = _rms(o, ng_ref[...]) * (1.0 - lam_init)
            o_ref[0, kh * half:(kh + 1) * half, :] = o.astype(BF16)


def _decode_diff(page_table, q_rows, k_new, v_new, lam_vecs, norm_g, cache_k, cache_v,
                 layer, t_len, lam_init, pps):
    db, rows, _ = q_rows.shape
    n_pages = page_table.shape[1]
    prow = DIFF_KVH * PAGE
    const = lambda shape: pl.BlockSpec(shape, lambda b, c, pt_ref: (0,) * len(shape))
    per_b = lambda shape: pl.BlockSpec(shape, lambda b, c, pt_ref: (b, 0, 0))
    hbm = pl.BlockSpec(memory_space=pl.ANY)
    grid_spec = pltpu.PrefetchScalarGridSpec(
        num_scalar_prefetch=1,
        grid=(db, n_pages // pps),
        in_specs=[per_b((1, rows, LANE)), per_b((1, prow, LANE)), per_b((1, prow, LANE)),
                  const((1, DIFF_HD)), const((1, DIFF_HD)), const((1, DIFF_HD)), const((1, DIFF_HD)),
                  const((1, LANE)), hbm, hbm],
        out_specs=per_b((1, rows, LANE)),
        scratch_shapes=[pltpu.VMEM((2, pps * prow, LANE), F32), pltpu.VMEM((2, pps * prow, LANE), F32),
                        pltpu.SemaphoreType.DMA((2, 2)),
                        pltpu.VMEM((2 * rows, LANE), BF16), pltpu.VMEM((2 * rows, 1), F32),
                        pltpu.VMEM((2 * rows, 1), F32), pltpu.VMEM((2 * rows, LANE), F32)],
    )
    return pl.pallas_call(
        functools.partial(_decode_diff_kernel, pps=pps, n_pages=n_pages, layer=layer, t_len=t_len,
                          lam_init=lam_init),
        grid_spec=grid_spec,
        out_shape=jax.ShapeDtypeStruct((db, rows, LANE), BF16),
        compiler_params=_cparams(("arbitrary", "arbitrary"), 56),
        name="decode_diff",
    )(page_table.reshape(-1), q_rows, k_new, v_new, *lam_vecs, norm_g, cache_k, cache_v)


def _odd_in_kernel(x_ref, w_ref, q_ref, k_ref, v_ref, kb_ref, vb_ref, vt_ref):
    h = _dot(x_ref[...].astype(BF16), w_ref[...])
    nq = DIFF_HEADS * 2 * DIFF_HD
    nkv = DIFF_KVH * 2 * DIFF_HD
    q_ref[...] = (h[:, :nq] * (DIFF_SCALE * LOG2E)).astype(BF16)
    k = h[:, nq:nq + nkv]
    v = h[:, nq + nkv:]
    tm = k.shape[0]
    for kh in range(DIFF_KVH):
        k_ref[pl.ds(kh, tm, stride=DIFF_KVH), :] = k[:, kh * LANE:(kh + 1) * LANE]
        v_ref[pl.ds(kh, tm, stride=DIFF_KVH), :] = v[:, kh * LANE:(kh + 1) * LANE]
        vt_ref[0, kh, 0] = v[:, kh * LANE:(kh + 1) * LANE].T.astype(BF16)
    kb_ref[...] = k.astype(BF16)
    vb_ref[...] = v.astype(BF16)


def _odd_in(x2d, w_in, tm, nblk):
    m = x2d.shape[0]
    nq = DIFF_HEADS * 2 * DIFF_HD
    nkv = DIFF_KVH * 2 * DIFF_HD
    row = lambda w: pl.BlockSpec((tm, w), lambda i: (i, 0))
    return pl.pallas_call(
        _odd_in_kernel,
        grid=(m // tm,),
        in_specs=[row(D_MODEL), _const_spec((D_MODEL, nq + 2 * nkv))],
        out_specs=[row(nq), pl.BlockSpec((DIFF_KVH * tm, LANE), lambda i: (i, 0)),
                   pl.BlockSpec((DIFF_KVH * tm, LANE), lambda i: (i, 0)), row(nkv), row(nkv),
                   pl.BlockSpec((1, DIFF_KVH, 1, LANE, tm), lambda i: (i // nblk, 0, i % nblk, 0, 0))],
        out_shape=[jax.ShapeDtypeStruct((m, nq), BF16),
                   jax.ShapeDtypeStruct((DIFF_KVH * m, LANE), F32), jax.ShapeDtypeStruct((DIFF_KVH * m, LANE), F32),
                   jax.ShapeDtypeStruct((m, nkv), BF16), jax.ShapeDtypeStruct((m, nkv), BF16),
                   jax.ShapeDtypeStruct((m // (nblk * tm), DIFF_KVH, nblk, LANE, tm), BF16)],
        compiler_params=_cparams(("arbitrary",), 40),
        name="odd_in",
    )(x2d, w_in)


def _post_kernel(*refs, even):
    if even:
        (x_ref, a_ref, pool_ref, wuv_ref, wmix_ref, g1_ref, b1_ref, g2_ref, b2_ref,
         w1_ref, fb1_ref, w2_ref, fb2_ref, o_ref) = refs
        o = _dot(a_ref[...], wuv_ref[...]).astype(BF16)
        mix = _dot(pool_ref[...], wmix_ref[:POOL_DIM, :]) + _dot(o, wmix_ref[POOL_DIM:, :])
    else:
        (x_ref, a_ref, wmix_ref, g1_ref, b1_ref, g2_ref, b2_ref,
         w1_ref, fb1_ref, w2_ref, fb2_ref, o_ref) = refs
        mix = _dot(a_ref[...], wmix_ref[...])
    x1 = _ln(ALPHA * x_ref[...] + mix, g1_ref[...], b1_ref[...])
    x1b = x1.astype(BF16)
    ffn = None
    for c in range(D_FF // FF_CHUNK):
        cols = slice(c * FF_CHUNK, (c + 1) * FF_CHUNK)
        hdn = jnp.square(jnp.maximum(_dot(x1b, w1_ref[:, cols]) + fb1_ref[:, cols], 0.0))
        part = _dot(hdn.astype(BF16), w2_ref[cols, :])
        ffn = part if ffn is None else ffn + part
    o_ref[...] = _ln(ALPHA * x1 + (ffn + fb2_ref[...]), g2_ref[...], b2_ref[...])


def _post(x2d, a, extra, wts, even, tm):
    m = x2d.shape[0]
    row = lambda w: pl.BlockSpec((tm, w), lambda i: (i, 0))
    vec = _const_spec((1, D_MODEL))
    ins = [x2d, a]
    specs = [row(D_MODEL), row(a.shape[1])]
    if even:
        ins += [extra, wts["w_uv"]]
        specs += [row(POOL_DIM), _const_spec((MLA_HEADS * LANE, MLA_HEADS * MLA_V))]
    ins += [wts["w_mix"], wts["g1"], wts["b1"], wts["g2"], wts["b2"],
            wts["w1"], wts["fb1"], wts["w2"], wts["fb2"]]
    layer = wts["layer"]
    stacked = lambda r, c: pl.BlockSpec((None, r, c), lambda i: (layer, 0, 0), pipeline_mode=pl.Buffered(1))
    specs += [_const_spec((D_MODEL, D_MODEL)), vec, vec, vec, vec,
              stacked(D_MODEL, D_FF), _const_spec((1, D_FF)), stacked(D_FF, D_MODEL), vec]
    return pl.pallas_call(
        functools.partial(_post_kernel, even=even),
        grid=(m // tm,),
        in_specs=specs,
        out_specs=row(D_MODEL),
        out_shape=jax.ShapeDtypeStruct((m, D_MODEL), F32),
        compiler_params=_cparams(("arbitrary",), 56),
        name="post_even" if even else "post_odd",
    )(*ins)


def _rope_tables(pos):
    half = MLA_ROPE // 2
    freq = ROPE_THETA ** (-jnp.arange(half, dtype=F32) / half)
    ang = pos[:, None] * freq[None, :]
    cos, sin = jnp.cos(ang), jnp.sin(ang)
    cc = jnp.concatenate([cos, cos], -1)
    ss = jnp.concatenate([-sin, sin], -1)
    pad = jnp.zeros((pos.shape[0], LANE - MLA_ROPE), F32)
    return (jnp.tile(cc, (1, MLA_HEADS)), jnp.tile(ss, (1, MLA_HEADS)),
            jnp.concatenate([cc, pad], -1), jnp.concatenate([ss, pad], -1))


def _block_diag(blocks):
    h, r, c = blocks.shape
    eye = jnp.eye(h, dtype=blocks.dtype)
    return (blocks[:, :, None, :] * eye[:, None, :, None]).reshape(h * r, h * c)


def _even_weights(w_in, q_norm, w_uq, kv_norm, w_uk, w_uv, w_out):
    half = MLA_ROPE // 2
    kr0 = POOL_DIM + MLA_Q_RANK + MLA_KV_RANK
    zpad = jnp.zeros((D_MODEL, LANE - MLA_ROPE), F32)
    w_in_p = jnp.concatenate(
        [w_in, zpad, w_in[:, kr0 + half:], w_in[:, kr0:kr0 + half], zpad], -1).astype(BF16)
    uq = w_uq.reshape(MLA_Q_RANK, MLA_HEADS, MLA_NOPE + MLA_ROPE)
    nope = uq[:, :, :MLA_NOPE].reshape(MLA_Q_RANK, MLA_HEADS * MLA_NOPE)
    rope = uq[:, :, MLA_NOPE:]
    rope_a = rope.reshape(MLA_Q_RANK, MLA_HEADS * MLA_ROPE)
    rope_b = jnp.concatenate([rope[:, :, half:], rope[:, :, :half]], -1).reshape(MLA_Q_RANK, MLA_HEADS * MLA_ROPE)
    place = jnp.concatenate([jnp.eye(MLA_ROPE, dtype=F32), jnp.zeros((MLA_ROPE, LANE - MLA_ROPE), F32)], -1)
    return {
        "w_in": w_in_p,
        "q_norm": q_norm.reshape(1, -1),
        "w_uq": jnp.concatenate([nope, rope_a, rope_b], -1).astype(BF16),
        "kv_norm": kv_norm.reshape(1, -1),
        "w_bd": _block_diag(jnp.transpose(w_uk, (1, 2, 0))).astype(BF16),
        "place": _block_diag(jnp.broadcast_to(place, (MLA_HEADS,) + place.shape)).astype(BF16),
        "w_uv": _block_diag(jnp.transpose(w_uv, (1, 0, 2))).astype(BF16),
        "w_mix": w_out.astype(BF16),
    }


def _ffn_weights(layer, g1, b1, g2, b2, w1_all, fb1, w2_all, fb2):
    r = lambda v: v[layer].reshape(1, -1)
    return {"layer": layer, "g1": r(g1), "b1": r(b1), "g2": r(g2), "b2": r(b2),
            "w1": w1_all, "fb1": r(fb1), "w2": w2_all, "fb2": r(fb2)}


def _token_tile(m):
    return TOKEN_TILE if m % TOKEN_TILE == 0 else m


def kernel(x_prompt, x_sample, cache_mla_ckv, cache_mla_krope, cache_diff_k, cache_diff_v, state_pool,
           page_table, w_in_even, pool_w, pool_scale, mla_q_norm, mla_w_uq, mla_kv_norm, mla_w_uk,
           mla_w_uv, w_out_even, w_in_odd, diff_lq1, diff_lk1, diff_lq2, diff_lk2, diff_norm, w_out_odd,
           ln1_g, ln1_b, ln2_g, ln2_b, mlp_w1, mlp_b1, mlp_w2, mlp_b2):
    bsz, s_len, d = x_prompt.shape
    db, t_len, _ = x_sample.shape
    n_pages = page_table.shape[1]
    past = n_pages * PAGE
    mp, ms = bsz * s_len, db * t_len
    tmp, tms = _token_tile(mp), _token_tile(ms)
    tq = tmp
    pps_mla = min(MLA_PAGES_PER_STEP, n_pages)
    pps_diff = min(DIFF_PAGES_PER_STEP, n_pages)
    n_pool = cache_diff_k.shape[1]
    kvw = DIFF_KVH * 2 * DIFF_HD
    cache_dk = cache_diff_k.reshape(cache_diff_k.shape[0], n_pool, DIFF_KVH * PAGE, LANE)
    cache_dv = cache_diff_v.reshape(cache_diff_v.shape[0], n_pool, DIFF_KVH * PAGE, LANE)
    cache_krt = jnp.swapaxes(cache_mla_krope, 2, 3)

    tabs_p = _rope_tables(jnp.arange(s_len, dtype=F32))
    tabs_s = tuple(jnp.tile(t, (tms // t_len, 1)) for t in _rope_tables(past + jnp.arange(t_len, dtype=F32)))
    pad_rows = lambda a, n: jnp.pad(a, ((0, 0), (0, n - a.shape[1]), (0, 0)))
    hist = 24 - t_len

    xp = x_prompt.reshape(mp, d)
    xs = x_sample.reshape(ms, d)
    outs = {k: [] for k in ("p_ckv", "p_kr", "p_dk", "p_dv", "p_pool", "s_ckv", "s_kr", "s_dk", "s_dv", "s_pool")}
    w1_all, w2_all = mlp_w1.astype(BF16), mlp_w2.astype(BF16)
    for i in range(DEPTH):
        ffn = _ffn_weights(i, ln1_g, ln1_b, ln2_g, ln2_b, w1_all, mlp_b1, w2_all, mlp_b2)
        if i % 2 == 0:
            e = i // 2
            wts = _even_weights(w_in_even[e], mla_q_norm[e], mla_w_uq[e], mla_kv_norm[e], mla_w_uk[e],
                                mla_w_uv[e], w_out_even[e])
            wts.update(ffn)
            pw = pool_w[e].astype(BF16)
            psc = pool_scale[e].reshape(1, -1)
            u, qcat, kcat, ckv, kr, vt = _even_in(xp, wts, tabs_p, s_len // tmp, tmp)
            u3 = u.reshape(bsz, s_len, POOL_DIM)
            pool_o = _pool(u3, pw, psc, 0, 1).reshape(mp, POOL_DIM)
            o_lat = _flash_mla(qcat.reshape(bsz, s_len, -1), kcat.reshape(bsz, s_len, QCAT), vt, tq)
            xp = _post(xp, o_lat.reshape(mp, -1), pool_o, wts, True, tmp)
            outs["p_ckv"].append(ckv.reshape(bsz, s_len, -1))
            outs["p_kr"].append(kr.reshape(bsz, s_len, -1))
            outs["p_pool"].append(u3[:, s_len - POOL_STATE:])
            u, qcat, kcat, ckv, kr, _ = _even_in(xs, wts, tabs_s, 1, tms)
            u3 = u.reshape(db, t_len, POOL_DIM)
            u_ext = jnp.concatenate([state_pool[e], u3], axis=1)
            u_pad = jnp.pad(u_ext, ((0, 0), (24 - u_ext.shape[1], 0), (0, 0)))
            bb = db
            while bb * u_pad.shape[1] > 2048 and bb % 2 == 0:
                bb //= 2
            pool_o = _pool(u_pad, pw, psc, past - hist, bb)[:, hist:].reshape(ms, POOL_DIM)
            q_rows = jnp.transpose(qcat.reshape(db, t_len, MLA_HEADS, QCAT), (0, 2, 1, 3))
            o_rows = _decode_mla(page_table, q_rows.reshape(db, MLA_HEADS * t_len, QCAT),
                                 pad_rows(kcat.reshape(db, t_len, QCAT), PAGE),
                                 cache_mla_ckv, cache_krt, e, t_len, pps_mla)
            o_lat = jnp.transpose(o_rows.reshape(db, MLA_HEADS, t_len, LANE), (0, 2, 1, 3)).reshape(ms, -1)
            xs = _post(xs, o_lat, pool_o, wts, True, tms)
            outs["s_ckv"].append(ckv.reshape(db, t_len, -1))
            outs["s_kr"].append(kr.reshape(db, t_len, -1))
            outs["s_pool"].append(u_ext[:, u_ext.shape[1] - POOL_STATE:])
        else:
            o = i // 2
            lam_init = 0.8 - 0.6 * math.exp(-0.3 * i)
            lam_vecs = tuple(v[o].reshape(1, -1) for v in (diff_lq1, diff_lk1, diff_lq2, diff_lk2))
            norm_g = diff_norm[o].reshape(1, -1)
            wts = dict(ffn)
            wts["w_mix"] = w_out_odd[o].astype(BF16)
            w_in = w_in_odd[o].astype(BF16)
            q, k, v, kb, _, vt = _odd_in(xp, w_in, tmp, s_len // tmp)
            att = _flash_diff(q.reshape(bsz, s_len, -1), kb.reshape(bsz, s_len, kvw), vt,
                              lam_vecs, norm_g.reshape(-1, 1), lam_init, tq)
            xp = _post(xp, att.reshape(mp, -1), None, wts, False, tmp)
            outs["p_dk"].append(k.reshape(bsz, s_len, DIFF_KVH, 2 * DIFF_HD))
            outs["p_dv"].append(v.reshape(bsz, s_len, DIFF_KVH, 2 * DIFF_HD))
            q, k, v, kb, vb, _ = _odd_in(xs, w_in, tms, 1)
            heads = DIFF_KVH * DIFF_GROUP
            q_rows = jnp.transpose(q.reshape(db, t_len, heads, LANE), (0, 2, 1, 3)).reshape(db, heads * t_len, LANE)
            new_rows = lambda a: pad_rows(a.reshape(db, t_len * DIFF_KVH, LANE), DIFF_KVH * PAGE)
            o_rows = _decode_diff(page_table, q_rows, new_rows(kb), new_rows(vb), lam_vecs, norm_g,
                                  cache_dk, cache_dv, o, t_len, lam_init, pps_diff)
            att = jnp.transpose(o_rows.reshape(db, heads, t_len, LANE), (0, 2, 1, 3)).reshape(ms, -1)
            xs = _post(xs, att, None, wts, False, tms)
            outs["s_dk"].append(k.reshape(db, t_len, DIFF_KVH, 2 * DIFF_HD))
            outs["s_dv"].append(v.reshape(db, t_len, DIFF_KVH, 2 * DIFF_HD))

    st = lambda name: jnp.stack(outs[name])
    return (xp.reshape(bsz, s_len, d), xs.reshape(db, t_len, d),
            st("p_ckv"), st("p_kr"), st("p_dk"), st("p_dv"), st("p_pool"),
            st("s_ckv"), st("s_kr"), st("s_dk"), st("s_dv"), st("s_pool"))
```
